```python
import math
import jax, jax.numpy as jnp
from jax import lax
import numpy as np

D_MODEL = 2048
BATCH = 8
SEQ = 2048
DEPTH = 4
DEC_BATCH = 4
DEC_SEQ = 4096
PAST_LEN = 128

GRID_W = 64
N_MIXERS = 2
N_ATTN_LAYERS = (DEPTH + 1) // 2
N_MLSTM_LAYERS = DEPTH // 2
HEAD_DIM = 128
N_Q_HEADS = D_MODEL // HEAD_DIM
N_KV_HEADS = N_Q_HEADS // 4
Q_BLOCK = 128
ROPE_THETA = 10000.0
ATTN_IN = (N_Q_HEADS + 2 * N_KV_HEADS) * HEAD_DIM
ML_HEADS = 8
ML_DV = D_MODEL // ML_HEADS
ML_DQK = ML_DV // 2
ML_CHUNK = 64
ML_GATES = 4 * ML_HEADS
ML_IN = 2 * ML_HEADS * ML_DQK + 2 * D_MODEL + ML_GATES
MEM_LEN = 256
XA_HEADS = 4
XA_HEAD_DIM = D_MODEL // XA_HEADS
D_FF = 4 * D_MODEL
DN_ALPHA = (2 * DEPTH) ** 0.25
DN_BETA = (8 * DEPTH) ** -0.25
LN_EPS = 1e-5
RMS_EPS = 1e-6

kernel_name = "hybrid_gqa_mlstm_deepnorm_encoder"

F32 = jnp.float32


def _layer_norm(x, g, b):
    xf = x.astype(F32)
    mu = xf.mean(-1, keepdims=True)
    var = jnp.square(xf - mu).mean(-1, keepdims=True)
    return ((xf - mu) * lax.rsqrt(var + LN_EPS) * g.astype(F32) + b.astype(F32)).astype(x.dtype)


def _rms_norm(x, g):
    xf = x.astype(F32)
    return (xf * lax.rsqrt(jnp.mean(xf * xf, -1, keepdims=True) + RMS_EPS) * g.astype(F32)).astype(x.dtype)


def _axial_rope_tables(S):
    rows = S // GRID_W
    row_ids = jnp.repeat(jnp.arange(rows), GRID_W).astype(F32)
    col_ids = jnp.tile(jnp.arange(GRID_W), rows).astype(F32)
    axis_dim = HEAD_DIM // 2
    inv_freq = ROPE_THETA ** (-jnp.arange(0, axis_dim, 2, dtype=F32) / axis_dim)
    ang = jnp.stack([row_ids[:, None] * inv_freq, col_ids[:, None] * inv_freq], axis=1)
    return jnp.cos(ang), jnp.sin(ang)


def _apply_axial_rope(x, cos, sin):
    B, S, H, _ = x.shape
    xs = x.astype(F32).reshape(B, S, H, 2, 2, HEAD_DIM // 4)
    x1, x2 = xs[..., 0, :], xs[..., 1, :]
    c, s = cos[:, None], sin[:, None]
    out = jnp.stack([x1 * c - x2 * s, x1 * s + x2 * c], axis=-2)
    return out.reshape(B, S, H, HEAD_DIM).astype(x.dtype)


def _gqa_axial(x, w_in, q_gain, k_gain, w_out):
    B, S, _ = x.shape
    h = x @ w_in
    q, k, v = jnp.split(h, [N_Q_HEADS * HEAD_DIM, (N_Q_HEADS + N_KV_HEADS) * HEAD_DIM], axis=-1)
    q = q.reshape(B, S, N_Q_HEADS, HEAD_DIM)
    k = k.reshape(B, S, N_KV_HEADS, HEAD_DIM)
    v = v.reshape(B, S, N_KV_HEADS, HEAD_DIM)
    cos, sin = _axial_rope_tables(S)
    q = _apply_axial_rope(_rms_norm(q, q_gain), cos, sin)
    k = _apply_axial_rope(_rms_norm(k, k_gain), cos, sin)
    G = N_Q_HEADS // N_KV_HEADS
    q = q.transpose(0, 2, 1, 3).reshape(B, N_KV_HEADS, G, S, HEAD_DIM)
    k = k.transpose(0, 2, 1, 3)
    v = v.transpose(0, 2, 1, 3)
    nb = S // Q_BLOCK
    qb = jnp.moveaxis(q.reshape(B, N_KV_HEADS, G, nb, Q_BLOCK, HEAD_DIM), 3, 0)
    scale = HEAD_DIM ** -0.5

    def block(qblk):
        s = jnp.einsum('bkgqd,bksd->bkgqs', qblk, k, preferred_element_type=F32) * scale
        p = jax.nn.softmax(s, axis=-1).astype(v.dtype)
        return jnp.einsum('bkgqs,bksd->bkgqd', p, v)

    o = lax.map(block, qb)
    o = jnp.moveaxis(o, 0, 3).reshape(B, N_Q_HEADS, S, HEAD_DIM)
    o = o.transpose(0, 2, 1, 3).reshape(B, S, N_Q_HEADS * HEAD_DIM)
    return o @ w_out


def _mlstm_chunkwise(q, k, v, log_i, log_f):
    B, H, S, DK = q.shape
    DV = v.shape[-1]
    nc = S // ML_CHUNK

    def chunks(a):
        a = a.reshape(B, H, nc, ML_CHUNK, *a.shape[3:])
        return jnp.moveaxis(a, 2, 0)

    xs = tuple(chunks(a) for a in (q, k, v, log_i, log_f))
    lower = jnp.tril(jnp.ones((ML_CHUNK, ML_CHUNK), dtype=bool))

    def step(carry, inp):
        C, n, m = carry
        qj, kj, vj, ij, fj = inp
        b = jnp.cumsum(fj, axis=-1)
        d = b[..., :, None] - b[..., None, :] + ij[..., None, :]
        d = jnp.where(lower, d, -jnp.inf)
        inter = b + m[..., None]
        m_j = jnp.maximum(inter, d.max(-1))
        w = jnp.exp(d - m_j[..., None])
        g = jnp.exp(inter - m_j)
        s = jnp.einsum('bhld,bhsd->bhls', qj, kj) * w
        num = g[..., None] * jnp.einsum('bhvd,bhld->bhlv', C, qj) + jnp.einsum('bhls,bhsv->bhlv', s, vj)
        den = g * jnp.einsum('bhd,bhld->bhl', n, qj) + s.sum(-1)
        h = num / jnp.maximum(jnp.abs(den), jnp.exp(-m_j))[..., None]
        bL = b[..., -1]
        dl = bL[..., None] - b + ij
        m_new = jnp.maximum(bL + m, dl.max(-1))
        gs = jnp.exp(bL + m - m_new)
        ws = jnp.exp(dl - m_new[..., None])
        C = gs[..., None, None] * C + jnp.einsum('bhs,bhsv,bhsd->bhvd', ws, vj, kj)
        n = gs[..., None] * n + jnp.einsum('bhs,bhsd->bhd', ws, kj)
        return (C, n, m_new), h

    init = (jnp.zeros((B, H, DV, DK), F32), jnp.zeros((B, H, DK), F32), jnp.zeros((B, H), F32))
    _, hs = lax.scan(step, init, xs)
    return jnp.moveaxis(hs, 0, 2).reshape(B, H, S, DV)


def _mlstm_bidir(x, w_in, b_gate, head_gain, w_out):
    B, S, _ = x.shape
    h = x @ w_in
    nqk = ML_HEADS * ML_DQK
    q, k, v, o, gates = jnp.split(h, [nqk, 2 * nqk, 2 * nqk + D_MODEL, 2 * nqk + 2 * D_MODEL], axis=-1)

    def to_heads(a, d):
        return a.reshape(B, S, ML_HEADS, d).transpose(0, 2, 1, 3).astype(F32)

    q = to_heads(q, ML_DQK)
    k = to_heads(k, ML_DQK) * (ML_DQK ** -0.5)
    v = to_heads(v, ML_DV)
    gates = (gates.reshape(B, S, 4, ML_HEADS) + b_gate).astype(F32).transpose(2, 0, 3, 1)
    log_i_f, log_f_f = gates[0], jax.nn.log_sigmoid(gates[1])
    log_i_b, log_f_b = gates[2], jax.nn.log_sigmoid(gates[3])
    h_f = _mlstm_chunkwise(q, k, v, log_i_f, log_f_f)

    def flip(a):
        return jnp.flip(a, axis=2)

    h_b = flip(_mlstm_chunkwise(flip(q), flip(k), flip(v), flip(log_i_b), flip(log_f_b)))
    hs = (h_f + h_b).transpose(0, 2, 1, 3)
    hs = _rms_norm(hs, head_gain).reshape(B, S, D_MODEL)
    hs = hs * jax.nn.sigmoid(o.astype(F32))
    return hs.astype(x.dtype) @ w_out


def _mem_cross_attn(x, mem, w_q, w_kv, w_out):
    B, S, _ = x.shape
    M = mem.shape[1]
    q = (x @ w_q).reshape(B, S, XA_HEADS, XA_HEAD_DIM)
    kv = (mem @ w_kv).reshape(B, M, 2, XA_HEADS, XA_HEAD_DIM)
    s = jnp.einsum('bshd,bmhd->bhsm', q, kv[:, :, 0], preferred_element_type=F32) * (XA_HEAD_DIM ** -0.5)
    p = jax.nn.softmax(s, axis=-1).astype(x.dtype)
    o = jnp.einsum('bhsm,bmhd->bshd', p, kv[:, :, 1]).reshape(B, S, D_MODEL)
    return o @ w_out


def _sq_relu_mlp(x, w1, w2):
    return jnp.square(jax.nn.relu(x @ w1)) @ w2


def _trunk(x, mem, p):
    for i in range(DEPTH):
        j = i // N_MIXERS
        if i % N_MIXERS == 0:
            y = _gqa_axial(x, p['attn_w_in'][j], p['attn_q_gain'][j], p['attn_k_gain'][j], p['attn_w_out'][j])
        else:
            y = _mlstm_bidir(x, p['ml_w_in'][j], p['ml_b_gate'][j], p['ml_head_gain'][j], p['ml_w_out'][j])
        x = _layer_norm(DN_ALPHA * x + y, p['ln_g'][i, 0], p['ln_b'][i, 0])
        y = _mem_cross_attn(x, mem, p['xa_w_q'][i], p['xa_w_kv'][i], p['xa_w_out'][i])
        x = _layer_norm(DN_ALPHA * x + y, p['ln_g'][i, 1], p['ln_b'][i, 1])
        y = _sq_relu_mlp(x, p['mlp_w1'][i], p['mlp_w2'][i])
        x = _layer_norm(DN_ALPHA * x + y, p['ln_g'][i, 2], p['ln_b'][i, 2])
    return x


def _normal(key, shape, scale):
    return jax.random.normal(key, shape, F32) * scale


def setup_inputs(seed: int = 0) -> dict:
    key = jax.random.key(seed)
    ks = jax.random.split(key, 19)
    D = D_MODEL
    gate_base = jnp.array([0.0, 3.0, 0.0, 3.0], F32)[None, :, None]
    return {
        'x_prompt': _normal(ks[0], (BATCH, SEQ, D), 1.0),
        'x_sample': _normal(ks[1], (DEC_BATCH, DEC_SEQ, D), 1.0),
        'mem_prompt': _normal(ks[2], (BATCH, MEM_LEN, D), 1.0),
        'mem_sample': _normal(ks[3], (DEC_BATCH, MEM_LEN, D), 1.0),
        'attn_w_in': _normal(ks[4], (N_ATTN_LAYERS, D, ATTN_IN), D ** -0.5),
        'attn_q_gain': 1.0 + _normal(ks[5], (N_ATTN_LAYERS, HEAD_DIM), 0.02),
        'attn_k_gain': 1.0 + _normal(ks[6], (N_ATTN_LAYERS, HEAD_DIM), 0.02),
        'attn_w_out': _normal(ks[7], (N_ATTN_LAYERS, N_Q_HEADS * HEAD_DIM, D), DN_BETA * (N_Q_HEADS * HEAD_DIM) ** -0.5),
        'ml_w_in': _normal(ks[8], (N_MLSTM_LAYERS, D, ML_IN), D ** -0.5),
        'ml_b_gate': gate_base + _normal(ks[9], (N_MLSTM_LAYERS, 4, ML_HEADS), 0.1),
        'ml_head_gain': 1.0 + _normal(ks[10], (N_MLSTM_LAYERS, ML_HEADS, ML_DV), 0.02),
        'ml_w_out': _normal(ks[11], (N_MLSTM_LAYERS, D, D), DN_BETA * D ** -0.5),
        'xa_w_q': _normal(ks[12], (DEPTH, D, D), D ** -0.5),
        'xa_w_kv': _normal(ks[13], (DEPTH, D, 2 * D), D ** -0.5),
        'xa_w_out': _normal(ks[14], (DEPTH, D, D), DN_BETA * D ** -0.5),
        'mlp_w1': _normal(ks[15], (DEPTH, D, D_FF), D ** -0.5),
        'mlp_w2': _normal(ks[16], (DEPTH, D_FF, D), DN_BETA * D_FF ** -0.5),
        'ln_g': 1.0 + _normal(ks[17], (DEPTH, 3, D), 0.02),
        'ln_b': _normal(ks[18], (DEPTH, 3, D), 0.02),
    }


def reference(x_prompt, x_sample, mem_prompt, mem_sample, attn_w_in, attn_q_gain, attn_k_gain, attn_w_out,
              ml_w_in, ml_b_gate, ml_head_gain, ml_w_out, xa_w_q, xa_w_kv, xa_w_out, mlp_w1, mlp_w2,
              ln_g, ln_b):
    params = {
        'attn_w_in': attn_w_in, 'attn_q_gain': attn_q_gain, 'attn_k_gain': attn_k_gain,
        'attn_w_out': attn_w_out, 'ml_w_in': ml_w_in, 'ml_b_gate': ml_b_gate,
        'ml_head_gain': ml_head_gain, 'ml_w_out': ml_w_out, 'xa_w_q': xa_w_q, 'xa_w_kv': xa_w_kv,
        'xa_w_out': xa_w_out, 'mlp_w1': mlp_w1, 'mlp_w2': mlp_w2, 'ln_g': ln_g, 'ln_b': ln_b,
    }
    y_prompt = _trunk(x_prompt, mem_prompt, params)
    y_sample = _trunk(x_sample, mem_sample, params)
    return (y_prompt, y_sample)
```

```python
import functools
import math

import jax
import jax.numpy as jnp
from jax import lax
from jax.experimental import pallas as pl
from jax.experimental.pallas import tpu as pltpu

F32 = jnp.float32
BF16 = jnp.bfloat16

DEPTH = 4
HEAD_DIM = 128
GQA_GROUP = 4
GRID_W = 64
ROPE_THETA = 10000.0
ML_HEADS = 8
XA_HEADS = 4
DN_ALPHA = (2 * DEPTH) ** 0.25
LN_EPS = 1e-5
RMS_EPS = 1e-6

V7X_VMEM_BYTES = 64 * 1024 * 1024
VMEM_LIMIT_BYTES = V7X_VMEM_BYTES - 8 * 1024 * 1024
LANES = 128

ML_CHUNK = 128


def _cparams(*sem):
    return pltpu.CompilerParams(dimension_semantics=sem, vmem_limit_bytes=VMEM_LIMIT_BYTES)


def _resident(block_shape, index_map):
    return pl.BlockSpec(block_shape, index_map, pipeline_mode=pl.Buffered(1))


def _deepnorm_ln(res, y, g, b):
    z = DN_ALPHA * res + y
    mu = jnp.mean(z, axis=-1, keepdims=True)
    zc = z - mu
    var = jnp.mean(zc * zc, axis=-1, keepdims=True)
    return zc * lax.rsqrt(var + LN_EPS) * g + b


def _proj_kernel(x_ref, w_ref, *rest, n_extra, epilogue):
    extra = rest[:n_extra]
    o_ref = rest[n_extra]
    xb_ref = rest[n_extra + 1]

    @pl.when(pl.program_id(1) == 0)
    def _():
        xb_ref[...] = x_ref[...].astype(BF16)

    acc = jnp.dot(xb_ref[...], w_ref[...], preferred_element_type=F32)
    epilogue(acc, extra, o_ref)


def _proj(x, w, out_dtype, epilogue, extras=(), extra_specs=(), tm=1024, tn=1024, name="proj"):
    m, k = x.shape
    n = w.shape[1]
    tm = min(tm, m)
    tn = min(tn, n)
    kern = functools.partial(_proj_kernel, n_extra=len(extras), epilogue=epilogue)
    return pl.pallas_call(
        kern,
        grid=(m // tm, n // tn),
        in_specs=[pl.BlockSpec((tm, k), lambda i, j: (i, 0)),
                  pl.BlockSpec((k, tn), lambda i, j: (0, j))] + list(extra_specs),
        out_specs=pl.BlockSpec((tm, tn), lambda i, j: (i, j)),
        out_shape=jax.ShapeDtypeStruct((m, n), out_dtype),
        scratch_shapes=[pltpu.VMEM((tm, k), BF16)],
        compiler_params=_cparams("parallel", "arbitrary"),
        name=name,
    )(x, w, *extras)


def _plain_epilogue(acc, extra, o_ref):
    o_ref[...] = acc.astype(o_ref.dtype)


def _rope_tables(s):
    rows = s // GRID_W
    row_ids = jnp.repeat(jnp.arange(rows), GRID_W).astype(F32)
    col_ids = jnp.tile(jnp.arange(GRID_W), rows).astype(F32)
    axis_dim = HEAD_DIM // 2
    inv_freq = ROPE_THETA ** (-jnp.arange(0, axis_dim, 2, dtype=F32) / axis_dim)
    ang_r = row_ids[:, None] * inv_freq
    ang_c = col_ids[:, None] * inv_freq
    zeros = jnp.zeros_like(ang_r)
    cos = jnp.concatenate([jnp.cos(ang_r), jnp.cos(ang_r), jnp.cos(ang_c), jnp.cos(ang_c)], axis=1)
    sa = jnp.concatenate([-jnp.sin(ang_r), zeros, -jnp.sin(ang_c), zeros], axis=1)
    sb = jnp.concatenate([zeros, jnp.sin(ang_r), zeros, jnp.sin(ang_c)], axis=1)
    return cos, sa, sb


def _norm_rope(xh, gain, cos, sa, sb):
    ms = jnp.mean(xh * xh, axis=-1, keepdims=True)
    y = xh * lax.rsqrt(ms + RMS_EPS) * gain
    quarter = HEAD_DIM // 4
    return (y * cos + pltpu.roll(y, HEAD_DIM - quarter, 1) * sa + pltpu.roll(y, quarter, 1) * sb)


def _q_epilogue(acc, extra, o_ref):
    cos_ref, sa_ref, sb_ref, g_ref = extra
    cos, sa, sb, g = cos_ref[...], sa_ref[...], sb_ref[...], g_ref[...]
    scale = HEAD_DIM ** -0.5
    for h in range(acc.shape[1] // HEAD_DIM):
        sl = slice(h * HEAD_DIM, (h + 1) * HEAD_DIM)
        o_ref[:, sl] = (_norm_rope(acc[:, sl], g, cos, sa, sb) * scale).astype(o_ref.dtype)


def _kv_epilogue(acc, extra, o_ref):
    cos_ref, sa_ref, sb_ref, g_ref = extra
    cos, sa, sb, g = cos_ref[...], sa_ref[...], sb_ref[...], g_ref[...]
    half = acc.shape[1] // 2
    for h in range(half // HEAD_DIM):
        sl = slice(h * HEAD_DIM, (h + 1) * HEAD_DIM)
        o_ref[:, sl] = _norm_rope(acc[:, sl], g, cos, sa, sb).astype(o_ref.dtype)
    o_ref[:, half:] = acc[:, half:].astype(o_ref.dtype)


def _attn_proj(x, w, gain, tables, seq, epilogue, name, tm=1024):
    tm = min(tm, seq)
    nblk = seq // tm
    tab_spec = pl.BlockSpec((tm, HEAD_DIM), lambda i, j: (i % nblk, 0))
    g_spec = pl.BlockSpec((1, HEAD_DIM), lambda i, j: (0, 0))
    return _proj(x, w, BF16, epilogue, extras=(*tables, gain.reshape(1, HEAD_DIM)),
                 extra_specs=(tab_spec, tab_spec, tab_spec, g_spec), tm=tm, name=name)


def _flash_kernel(q_ref, k_ref, v_ref, o_ref, q4_ref, *, tq, tk):
    seq = k_ref.shape[0]
    for h in range(GQA_GROUP):
        q4_ref[h * tq:(h + 1) * tq, :] = q_ref[:, h * HEAD_DIM:(h + 1) * HEAD_DIM]
    q4 = q4_ref[...]
    rows = GQA_GROUP * tq

    def body(c, carry):
        m, l, acc = carry
        start = pl.multiple_of(c * tk, tk)
        kc = k_ref[pl.ds(start, tk), :]
        vc = v_ref[pl.ds(start, tk), :]
        s = lax.dot_general(q4, kc, (((1,), (1,)), ((), ())), preferred_element_type=F32)
        m_new = jnp.maximum(m, jnp.max(s, axis=-1, keepdims=True))
        p = jnp.exp(s - m_new)
        alpha = jnp.exp(m - m_new)
        l_new = alpha * l + jnp.sum(p, axis=-1, keepdims=True)
        acc_new = alpha * acc + jnp.dot(p.astype(BF16), vc, preferred_element_type=F32)
        return m_new, l_new, acc_new

    init = (jnp.full((rows, 1), -jnp.inf, F32), jnp.zeros((rows, 1), F32),
            jnp.zeros((rows, HEAD_DIM), F32))
    _, l, acc = lax.fori_loop(0, seq // tk, body, init)
    out = acc / l
    for h in range(GQA_GROUP):
        o_ref[:, h * HEAD_DIM:(h + 1) * HEAD_DIM] = out[h * tq:(h + 1) * tq, :].astype(o_ref.dtype)


def _flash_attention(q, kv, batch, seq, tq=256, tk=512):
    t, dq = q.shape
    nkv = dq // HEAD_DIM // GQA_GROUP
    tq = min(tq, seq)
    tk = min(tk, seq)
    nq = seq // tq
    gw = GQA_GROUP * HEAD_DIM
    kern = functools.partial(_flash_kernel, tq=tq, tk=tk)
    return pl.pallas_call(
        kern,
        grid=(batch, nkv, nq),
        in_specs=[pl.BlockSpec((tq, gw), lambda b, g, i: (b * nq + i, g)),
                  pl.BlockSpec((seq, HEAD_DIM), lambda b, g, i: (b, g)),
                  pl.BlockSpec((seq, HEAD_DIM), lambda b, g, i: (b, nkv + g))],
        out_specs=pl.BlockSpec((tq, gw), lambda b, g, i: (b * nq + i, g)),
        out_shape=jax.ShapeDtypeStruct((t, dq), BF16),
        scratch_shapes=[pltpu.VMEM((GQA_GROUP * tq, HEAD_DIM), BF16)],
        compiler_params=_cparams("parallel", "parallel", "arbitrary"),
        name="flash_attention",
    )(q, kv, kv)


def _out_ln_kernel(a_ref, w_ref, res_ref, g_ref, b_ref, o_ref):
    y = jnp.dot(a_ref[...], w_ref[...], preferred_element_type=F32)
    o_ref[...] = _deepnorm_ln(res_ref[...], y, g_ref[...], b_ref[...])


def _out_ln(a, w, res, g, b, tm=512):
    m, k = a.shape
    d = w.shape[1]
    tm = min(tm, m)
    return pl.pallas_call(
        _out_ln_kernel,
        grid=(m // tm,),
        in_specs=[pl.BlockSpec((tm, k), lambda i: (i, 0)),
                  _resident((k, d), lambda i: (0, 0)),
                  pl.BlockSpec((tm, d), lambda i: (i, 0)),
                  _resident((1, d), lambda i: (0, 0)),
                  _resident((1, d), lambda i: (0, 0))],
        out_specs=pl.BlockSpec((tm, d), lambda i: (i, 0)),
        out_shape=jax.ShapeDtypeStruct((m, d), F32),
        compiler_params=_cparams("parallel"),
        name="attn_out_ln",
    )(a, w, res, g.reshape(1, d), b.reshape(1, d))


def _ml_qkv_epilogue(acc, extra, o_ref, *, k_scale):
    scale = jnp.where(pl.program_id(1) == 1, k_scale, 1.0).astype(F32)
    o_ref[...] = (acc * scale).astype(o_ref.dtype)


def _log_sigmoid(x):
    return jnp.minimum(x, 0.0) - jnp.log(1.0 + jnp.exp(-jnp.abs(x)))


def _mlstm_kernel(q_ref, k_ref, v_ref, g_ref, bias_ref, o_ref, c_ref, n_ref, m_ref, *, dqk, dv):
    d = pl.program_id(1)
    chunk = q_ref.shape[0]

    @pl.when(pl.program_id(2) == 0)
    def _():
        c_ref[...] = jnp.zeros_like(c_ref)
        n_ref[...] = jnp.zeros_like(n_ref)
        m_ref[...] = jnp.zeros_like(m_ref)

    gates = g_ref[0, 0] + bias_ref[0]
    row = lax.broadcasted_iota(jnp.int32, (chunk, chunk), 0)
    col = lax.broadcasted_iota(jnp.int32, (chunk, chunk), 1)
    sgn = 1 - 2 * d
    mask = (row - col) * sgn >= 0
    eye = row == col

    for h in range(ML_HEADS):
        i_row = gates[h:h + 1, :]
        f_row = _log_sigmoid(gates[ML_HEADS + h:ML_HEADS + h + 1, :])
        m_prev = m_ref[h:h + 1, 0:1]
        n_prev = n_ref[h:h + 1, :]
        ct = c_ref[h]

        b_col = jnp.sum(jnp.where(mask, f_row, 0.0), axis=1, keepdims=True)
        b_row = jnp.sum(jnp.where(eye, b_col, 0.0), axis=0, keepdims=True)
        i_col = jnp.sum(jnp.where(eye, i_row, 0.0), axis=1, keepdims=True)
        a_row = i_row - b_row
        a_mat = jnp.where(mask, a_row, -jnp.inf)
        m_rows = jnp.maximum(m_prev, jnp.max(a_mat, axis=1, keepdims=True))
        w_mat = jnp.exp(a_mat - m_rows)
        g_col = jnp.exp(m_prev - m_rows)

        qh = q_ref[:, h * dqk:(h + 1) * dqk]
        kh = k_ref[:, h * dqk:(h + 1) * dqk]
        vh = v_ref[:, h * dv:(h + 1) * dv]
        s = lax.dot_general(qh, kh, (((1,), (1,)), ((), ())), preferred_element_type=F32) * w_mat
        inter = jnp.dot(qh, ct.astype(BF16), preferred_element_type=F32)
        num = g_col * inter + jnp.dot(s.astype(BF16), vh, preferred_element_type=F32)
        qn = jnp.sum(qh.astype(F32) * n_prev, axis=1, keepdims=True)
        den = g_col * qn + jnp.sum(s, axis=1, keepdims=True)
        o_ref[0, :, h * dv:(h + 1) * dv] = num / jnp.maximum(jnp.abs(den), jnp.exp(-(b_col + m_rows)))

        b_tot = jnp.sum(f_row, axis=1, keepdims=True)
        m_x = jnp.maximum(m_prev, jnp.max(a_row, axis=1, keepdims=True))
        g_s = jnp.exp(m_prev - m_x)
        ws_col = jnp.exp(i_col - b_col - m_x)
        ks = kh.astype(F32) * ws_col
        upd = lax.dot_general(ks.astype(BF16), vh, (((0,), (0,)), ((), ())),
                              preferred_element_type=F32)
        c_ref[h] = g_s * ct + upd
        n_ref[h:h + 1, :] = g_s * n_prev + jnp.sum(ks, axis=0, keepdims=True)
        m_ref[h:h + 1, :] = jnp.broadcast_to(b_tot + m_x, (1, m_ref.shape[1]))


def _mlstm(qkv, gates, bias, batch, seq):
    t = qkv.shape[0]
    dv = qkv.shape[1] // (2 * ML_HEADS)
    dqk = dv // 2
    chunk = min(ML_CHUNK, seq)
    nc = seq // chunk
    wqk = ML_HEADS * dqk
    wv = ML_HEADS * dv

    def rb(b, d, c):
        return b * nc + c + d * (nc - 1 - 2 * c)

    kern = functools.partial(_mlstm_kernel, dqk=dqk, dv=dv)
    return pl.pallas_call(
        kern,
        grid=(batch, 2, nc),
        in_specs=[pl.BlockSpec((chunk, wqk), lambda b, d, c: (rb(b, d, c), 0)),
                  pl.BlockSpec((chunk, wqk), lambda b, d, c: (rb(b, d, c), 1)),
                  pl.BlockSpec((chunk, wv), lambda b, d, c: (rb(b, d, c), 1)),
                  pl.BlockSpec((1, 1, 2 * ML_HEADS, chunk),
                               lambda b, d, c: (b, d, 0, c + d * (nc - 1 - 2 * c))),
                  pl.BlockSpec((1, 2 * ML_HEADS, 1), lambda b, d, c: (d, 0, 0))],
        out_specs=pl.BlockSpec((1, chunk, wv), lambda b, d, c: (d, rb(b, d, c), 0)),
        out_shape=jax.ShapeDtypeStruct((2, t, wv), F32),
        scratch_shapes=[pltpu.VMEM((ML_HEADS, dqk, dv), F32),
                        pltpu.VMEM((ML_HEADS, dqk), F32),
                        pltpu.VMEM((ML_HEADS, LANES), F32)],
        compiler_params=_cparams("parallel", "parallel", "arbitrary"),
        name="mlstm_chunks",
    )(qkv, qkv, qkv, gates, bias)


def _ml_out_kernel(hf_ref, hb_ref, og_ref, gain_ref, w_ref, res_ref, g_ref, b_ref, o_ref, a_ref, *, dv):
    for h in range(ML_HEADS):
        sl = slice(h * dv, (h + 1) * dv)
        hs = hf_ref[0, :, sl] + hb_ref[0, :, sl]
        ms = jnp.mean(hs * hs, axis=-1, keepdims=True)
        hn = hs * lax.rsqrt(ms + RMS_EPS) * gain_ref[:, sl]
        a_ref[:, sl] = (hn * jax.nn.sigmoid(og_ref[:, sl])).astype(BF16)
    y = jnp.dot(a_ref[...], w_ref[...], preferred_element_type=F32)
    o_ref[...] = _deepnorm_ln(res_ref[...], y, g_ref[...], b_ref[...])


def _ml_out(h2, og, gain, w, res, g, b, tm=256):
    _, m, d = h2.shape
    tm = min(tm, m)
    kern = functools.partial(_ml_out_kernel, dv=d // ML_HEADS)
    row = lambda i: (i, 0)
    const = lambda i: (0, 0)
    return pl.pallas_call(
        kern,
        grid=(m // tm,),
        in_specs=[pl.BlockSpec((1, tm, d), lambda i: (0, i, 0)),
                  pl.BlockSpec((1, tm, d), lambda i: (1, i, 0)),
                  pl.BlockSpec((tm, d), row),
                  _resident((1, d), const),
                  _resident((d, d), const),
                  pl.BlockSpec((tm, d), row),
                  _resident((1, d), const),
                  _resident((1, d), const)],
        out_specs=pl.BlockSpec((tm, d), row),
        out_shape=jax.ShapeDtypeStruct((m, d), F32),
        scratch_shapes=[pltpu.VMEM((tm, d), BF16)],
        compiler_params=_cparams("parallel"),
        name="mlstm_out_ln",
    )(h2, h2, og, gain.reshape(1, d), w, res, g.reshape(1, d), b.reshape(1, d))


def _xattn_kernel(x_ref, wq_ref, kv_ref, wo_ref, g_ref, b_ref, o_ref, a_ref):
    x = x_ref[...]
    d = x.shape[1]
    hd = d // XA_HEADS
    q = jnp.dot(x.astype(BF16), wq_ref[...], preferred_element_type=F32).astype(BF16)
    scale = hd ** -0.5
    for h in range(XA_HEADS):
        kh = kv_ref[:, h * hd:(h + 1) * hd]
        vh = kv_ref[:, d + h * hd:d + (h + 1) * hd]
        s = lax.dot_general(q[:, h * hd:(h + 1) * hd], kh, (((1,), (1,)), ((), ())),
                            preferred_element_type=F32) * scale
        e = jnp.exp(s - jnp.max(s, axis=-1, keepdims=True))
        p = e / jnp.sum(e, axis=-1, keepdims=True)
        a_ref[:, h * hd:(h + 1) * hd] = jnp.dot(p.astype(BF16), vh,
                                                preferred_element_type=F32).astype(BF16)
    y = jnp.dot(a_ref[...], wo_ref[...], preferred_element_type=F32)
    o_ref[...] = _deepnorm_ln(x, y, g_ref[...], b_ref[...])


def _xattn(x, wq, kv, wo, g, b, batch, seq, tq=512):
    t, d = x.shape
    mem_len = kv.shape[0] // batch
    tq = min(tq, seq)
    nq = seq // tq
    const = lambda bi, i: (0, 0)
    return pl.pallas_call(
        _xattn_kernel,
        grid=(batch, nq),
        in_specs=[pl.BlockSpec((tq, d), lambda bi, i: (bi * nq + i, 0)),
                  _resident((d, d), const),
                  pl.BlockSpec((mem_len, 2 * d), lambda bi, i: (bi, 0)),
                  _resident((d, d), const),
                  _resident((1, d), const),
                  _resident((1, d), const)],
        out_specs=pl.BlockSpec((tq, d), lambda bi, i: (bi * nq + i, 0)),
        out_shape=jax.ShapeDtypeStruct((t, d), F32),
        scratch_shapes=[pltpu.VMEM((tq, d), BF16)],
        compiler_params=_cparams("parallel", "arbitrary"),
        name="xattn_sublayer",
    )(x, wq, kv, wo, g.reshape(1, d), b.reshape(1, d))


def _mlp_kernel(x_ref, w1_ref, w2_ref, g_ref, b_ref, o_ref, xb_ref, acc_ref):
    f = pl.program_id(1)

    @pl.when(f == 0)
    def _():
        xb_ref[...] = x_ref[...].astype(BF16)
        acc_ref[...] = jnp.zeros_like(acc_ref)

    h = jnp.dot(xb_ref[...], w1_ref[...], preferred_element_type=F32)
    h = jnp.maximum(h, 0.0)
    acc_ref[...] += jnp.dot((h * h).astype(BF16), w2_ref[...], preferred_element_type=F32)

    @pl.when(f == pl.num_programs(1) - 1)
    def _():
        o_ref[...] = _deepnorm_ln(x_ref[...], acc_ref[...], g_ref[...], b_ref[...])


def _mlp(x, w1, w2, g, b, tm=512, tf=512):
    m, d = x.shape
    dff = w1.shape[1]
    tm = min(tm, m)
    tf = min(tf, dff)
    const = lambda i, f: (0, 0)
    return pl.pallas_call(
        _mlp_kernel,
        grid=(m // tm, dff // tf),
        in_specs=[pl.BlockSpec((tm, d), lambda i, f: (i, 0)),
                  pl.BlockSpec((d, tf), lambda i, f: (0, f)),
                  pl.BlockSpec((tf, d), lambda i, f: (f, 0)),
                  _resident((1, d), const),
                  _resident((1, d), const)],
        out_specs=pl.BlockSpec((tm, d), lambda i, f: (i, 0)),
        out_shape=jax.ShapeDtypeStruct((m, d), F32),
        scratch_shapes=[pltpu.VMEM((tm, d), BF16), pltpu.VMEM((tm, d), F32)],
        compiler_params=_cparams("parallel", "arbitrary"),
        name="mlp_sublayer",
    )(x, w1, w2, g.reshape(1, d), b.reshape(1, d))


def _gqa_sublayer(x, batch, seq, w_in, q_gain, k_gain, w_out, ln_g, ln_b):
    d = x.shape[1]
    tables = _rope_tables(seq)
    q = _attn_proj(x, w_in[:, :d].astype(BF16), q_gain, tables, seq, _q_epilogue, "attn_q_proj")
    kv = _attn_proj(x, w_in[:, d:].astype(BF16), k_gain, tables, seq, _kv_epilogue, "attn_kv_proj")
    o = _flash_attention(q, kv, batch, seq)
    return _out_ln(o, w_out.astype(BF16), x, ln_g, ln_b)


def _mlstm_sublayer(x, batch, seq, w_in, b_gate, head_gain, w_out, ln_g, ln_b):
    t, d = x.shape
    dv = d // ML_HEADS
    dqk = dv // 2
    nqk = ML_HEADS * dqk
    w_qkv = w_in[:, :2 * nqk + d].astype(BF16)
    w_o = w_in[:, 2 * nqk + d:2 * nqk + 2 * d].astype(BF16)
    ng = 4 * ML_HEADS
    w_g = jnp.pad(w_in[:, 2 * nqk + 2 * d:], ((0, 0), (0, LANES - ng))).astype(BF16)
    qkv = _proj(x, w_qkv, BF16, functools.partial(_ml_qkv_epilogue, k_scale=dqk ** -0.5), tn=nqk,
                name="mlstm_qkv_proj")
    og = _proj(x, w_o, F32, _plain_epilogue, name="mlstm_ogate_proj")
    gates = _proj(x, w_g, F32, _plain_epilogue, name="mlstm_gates_proj")
    gates = gates[:, :ng].reshape(batch, seq, 2, 2 * ML_HEADS).transpose(0, 2, 3, 1)
    bias = b_gate.astype(F32).reshape(2, 2 * ML_HEADS, 1)
    h2 = _mlstm(qkv, gates, bias, batch, seq)
    return _ml_out(h2, og, head_gain.reshape(-1), w_out.astype(BF16), x, ln_g, ln_b)


def _trunk(x3, mem3, p):
    batch, seq, d = x3.shape
    x = x3.reshape(batch * seq, d)
    mem = mem3.reshape(-1, d)
    for i in range(DEPTH):
        j = i // 2
        if i % 2 == 0:
            x = _gqa_sublayer(x, batch, seq, p['attn_w_in'][j], p['attn_q_gain'][j], p['attn_k_gain'][j],
                              p['attn_w_out'][j], p['ln_g'][i, 0], p['ln_b'][i, 0])
        else:
            x = _mlstm_sublayer(x, batch, seq, p['ml_w_in'][j], p['ml_b_gate'][j], p['ml_head_gain'][j],
                                p['ml_w_out'][j], p['ln_g'][i, 0], p['ln_b'][i, 0])
        kv = _proj(mem, p['xa_w_kv'][i].astype(BF16), BF16, _plain_epilogue, name="xattn_kv_proj")
        x = _xattn(x, p['xa_w_q'][i].astype(BF16), kv, p['xa_w_out'][i].astype(BF16),
                   p['ln_g'][i, 1], p['ln_b'][i, 1], batch, seq)
        x = _mlp(x, p['mlp_w1'][i].astype(BF16), p['mlp_w2'][i].astype(BF16),
                 p['ln_g'][i, 2], p['ln_b'][i, 2])
    return x.reshape(batch, seq, d)


def kernel(x_prompt, x_sample, mem_prompt, mem_sample, attn_w_in, attn_q_gain, attn_k_gain, attn_w_out,
           ml_w_in, ml_b_gate, ml_head_gain, ml_w_out, xa_w_q, xa_w_kv, xa_w_out, mlp_w1, mlp_w2,
           ln_g, ln_b):
    params = {
        'attn_w_in': attn_w_in, 'attn_q_gain': attn_q_gain, 'attn_k_gain': attn_k_gain,
        'attn_w_out': attn_w_out, 'ml_w_in': ml_w_in, 'ml_b_gate': ml_b_gate,
        'ml_head_gain': ml_head_gain, 'ml_w_out': ml_w_out, 'xa_w_q': xa_w_q, 'xa_w_kv': xa_w_kv,
        'xa_w_out': xa_w_out, 'mlp_w1': mlp_w1, 'mlp_w2': mlp_w2, 'ln_g': ln_g, 'ln_b': ln_b,
    }
    y_prompt = _trunk(x_prompt, mem_prompt, params)
    y_sample = _trunk(x_sample, mem_sample, params)
    return (y_prompt, y_sample)
```

```python
import functools
import math

import jax
import jax.numpy as jnp
from jax import lax
from jax.experimental import pallas as pl
from jax.experimental.pallas import tpu as pltpu

F32 = jnp.float32
BF16 = jnp.bfloat16

DEPTH = 4
HEAD_DIM = 128
GQA_GROUP = 4
GRID_W = 64
ROPE_THETA = 10000.0
ML_HEADS = 8
XA_HEADS = 4
DN_ALPHA = (2 * DEPTH) ** 0.25
LN_EPS = 1e-5
RMS_EPS = 1e-6

V7X_VMEM_BYTES = 64 * 1024 * 1024
VMEM_LIMIT_BYTES = V7X_VMEM_BYTES - 8 * 1024 * 1024
LANES = 128

ML_CHUNK = 128


def _cparams(*sem):
    return pltpu.CompilerParams(dimension_semantics=sem, vmem_limit_bytes=VMEM_LIMIT_BYTES)


def _resident(block_shape, index_map):
    return pl.BlockSpec(block_shape, index_map, pipeline_mode=pl.Buffered(1))


def _deepnorm_ln(res, y, g, b):
    z = DN_ALPHA * res + y
    mu = jnp.mean(z, axis=-1, keepdims=True)
    zc = z - mu
    var = jnp.mean(zc * zc, axis=-1, keepdims=True)
    return zc * lax.rsqrt(var + LN_EPS) * g + b


def _proj_kernel(x_ref, w_ref, *rest, n_extra, epilogue):
    extra = rest[:n_extra]
    o_ref = rest[n_extra]
    xb_ref = rest[n_extra + 1]

    @pl.when(pl.program_id(1) == 0)
    def _():
        xb_ref[...] = x_ref[...].astype(BF16)

    acc = jnp.dot(xb_ref[...], w_ref[...], preferred_element_type=F32)
    epilogue(acc, extra, o_ref)


def _proj(x, w, out_dtype, epilogue, extras=(), extra_specs=(), tm=1024, tn=1024, name="proj"):
    m, k = x.shape
    n = w.shape[1]
    tm = min(tm, m)
    tn = min(tn, n)
    kern = functools.partial(_proj_kernel, n_extra=len(extras), epilogue=epilogue)
    return pl.pallas_call(
        kern,
        grid=(m // tm, n // tn),
        in_specs=[pl.BlockSpec((tm, k), lambda i, j: (i, 0)),
                  pl.BlockSpec((k, tn), lambda i, j: (0, j))] + list(extra_specs),
        out_specs=pl.BlockSpec((tm, tn), lambda i, j: (i, j)),
        out_shape=jax.ShapeDtypeStruct((m, n), out_dtype),
        scratch_shapes=[pltpu.VMEM((tm, k), BF16)],
        compiler_params=_cparams("parallel", "arbitrary"),
        name=name,
    )(x, w, *extras)


def _plain_epilogue(acc, extra, o_ref):
    o_ref[...] = acc.astype(o_ref.dtype)


def _rope_tables(s):
    rows = s // GRID_W
    row_ids = jnp.repeat(jnp.arange(rows), GRID_W).astype(F32)
    col_ids = jnp.tile(jnp.arange(GRID_W), rows).astype(F32)
    axis_dim = HEAD_DIM // 2
    inv_freq = ROPE_THETA ** (-jnp.arange(0, axis_dim, 2, dtype=F32) / axis_dim)
    ang = jnp.concatenate([row_ids[:, None] * inv_freq, col_ids[:, None] * inv_freq], axis=1)
    cos = jnp.concatenate([jnp.cos(ang), jnp.cos(ang)], axis=1)
    sin = jnp.concatenate([-jnp.sin(ang), jnp.sin(ang)], axis=1)
    return cos, sin


def _half_split_heads(a):
    lead = a.shape[:-1]
    a = a.reshape(*lead, -1, 2, 2, HEAD_DIM // 4)
    return jnp.swapaxes(a, -3, -2).reshape(*lead, -1)


def _attn_proj_kernel(x_ref, w_ref, cos_ref, sin_ref, g_ref, gsw_ref, o_ref, xb_ref, acc_ref, *,
                      rope_heads, out_scale, rows):
    @pl.when(pl.program_id(1) == 0)
    def _():
        xb_ref[...] = x_ref[...].astype(BF16)

    acc_ref[...] = jnp.dot(xb_ref[...], w_ref[...], preferred_element_type=F32)
    tm, tn = acc_ref.shape
    g = g_ref[...]
    gsw = gsw_ref[...]

    def body(r, carry):
        r0 = pl.multiple_of(r * rows, rows)
        gc = cos_ref[pl.ds(r0, rows), :] * g
        gs = sin_ref[pl.ds(r0, rows), :] * gsw
        for h in range(rope_heads):
            sl = slice(h * HEAD_DIM, (h + 1) * HEAD_DIM)
            a = acc_ref[pl.ds(r0, rows), sl]
            ms = jnp.mean(a * a, axis=-1, keepdims=True)
            rr = lax.rsqrt(ms + RMS_EPS) * out_scale
            y = (a * gc + pltpu.roll(a, HEAD_DIM // 2, 1) * gs) * rr
            o_ref[pl.ds(r0, rows), sl] = y.astype(o_ref.dtype)
        return carry

    lax.fori_loop(0, tm // rows, body, 0, unroll=2)
    k_cols = rope_heads * HEAD_DIM
    for h in range((tn - k_cols) // HEAD_DIM):
        src = slice(k_cols + h * HEAD_DIM, k_cols + (h + 1) * HEAD_DIM)
        dst = k_cols + 2 * h * HEAD_DIM
        o_ref[:, dst:dst + HEAD_DIM] = acc_ref[:, src].astype(o_ref.dtype)
        o_ref[:, dst + HEAD_DIM:dst + 2 * HEAD_DIM] = jnp.ones((tm, HEAD_DIM), o_ref.dtype)


def _attn_proj(x, w, gain, tables, seq, rope_heads_per_block, out_scale, name, tm=1024, tn=1024, rows=128):
    m, k = x.shape
    n = w.shape[1]
    tm = min(tm, seq)
    tn = min(tn, n)
    rows = min(rows, tm)
    nblk = seq // tm
    t_out = rope_heads_per_block * HEAD_DIM + 2 * (tn - rope_heads_per_block * HEAD_DIM)
    cos, sin = tables
    gain = _half_split_heads(gain).reshape(1, HEAD_DIM)
    gain_sw = jnp.roll(gain, HEAD_DIM // 2, axis=1)
    kern = functools.partial(_attn_proj_kernel, rope_heads=rope_heads_per_block, out_scale=out_scale,
                             rows=rows)
    tab_spec = pl.BlockSpec((tm, HEAD_DIM), lambda i, j: (i % nblk, 0))
    g_spec = pl.BlockSpec((1, HEAD_DIM), lambda i, j: (0, 0))
    return pl.pallas_call(
        kern,
        grid=(m // tm, n // tn),
        in_specs=[pl.BlockSpec((tm, k), lambda i, j: (i, 0)),
                  pl.BlockSpec((k, tn), lambda i, j: (0, j)),
                  tab_spec, tab_spec, g_spec, g_spec],
        out_specs=pl.BlockSpec((tm, t_out), lambda i, j: (i, j)),
        out_shape=jax.ShapeDtypeStruct((m, (n // tn) * t_out), BF16),
        scratch_shapes=[pltpu.VMEM((tm, k), BF16), pltpu.VMEM((tm, tn), F32)],
        compiler_params=_cparams("parallel", "arbitrary"),
        name=name,
    )(x, w, cos, sin, gain, gain_sw)


def _flash_kernel(q_ref, k_ref, v_ref, o_ref, q4_ref, *, tq, tk):
    seq = k_ref.shape[0]
    for h in range(GQA_GROUP):
        q4_ref[h * tq:(h + 1) * tq, :] = q_ref[:, h * HEAD_DIM:(h + 1) * HEAD_DIM]
    q4 = q4_ref[...]
    rows = GQA_GROUP * tq

    m = jnp.full((rows, 1), -jnp.inf, F32)
    acc = jnp.zeros((rows, 2 * HEAD_DIM), F32)
    for c in range(seq // tk):
        kc = k_ref[c * tk:(c + 1) * tk, :]
        vc = v_ref[c * tk:(c + 1) * tk, :]
        s = lax.dot_general(q4, kc, (((1,), (1,)), ((), ())), preferred_element_type=F32)
        m_new = jnp.maximum(m, jnp.max(s, axis=-1, keepdims=True))
        p = jnp.exp2(s - m_new).astype(BF16)
        acc = jnp.exp2(m - m_new) * acc + jnp.dot(p, vc, preferred_element_type=F32)
        m = m_new
    out = acc[:, :HEAD_DIM] / acc[:, HEAD_DIM:]
    for h in range(GQA_GROUP):
        o_ref[:, h * HEAD_DIM:(h + 1) * HEAD_DIM] = out[h * tq:(h + 1) * tq, :].astype(o_ref.dtype)


def _flash_attention(q, kv, batch, seq, tq=512, tk=256):
    t, dq = q.shape
    nkv = dq // HEAD_DIM // GQA_GROUP
    tq = min(tq, seq)
    tk = min(tk, seq)
    nq = seq // tq
    gw = GQA_GROUP * HEAD_DIM
    v0 = nkv // 2
    kern = functools.partial(_flash_kernel, tq=tq, tk=tk)
    return pl.pallas_call(
        kern,
        grid=(batch, nkv, nq),
        in_specs=[pl.BlockSpec((tq, gw), lambda b, g, i: (b * nq + i, g)),
                  pl.BlockSpec((seq, HEAD_DIM), lambda b, g, i: (b, g)),
                  pl.BlockSpec((seq, 2 * HEAD_DIM), lambda b, g, i: (b, v0 + g))],
        out_specs=pl.BlockSpec((tq, gw), lambda b, g, i: (b * nq + i, g)),
        out_shape=jax.ShapeDtypeStruct((t, dq), BF16),
        scratch_shapes=[pltpu.VMEM((GQA_GROUP * tq, HEAD_DIM), BF16)],
        compiler_params=_cparams("parallel", "parallel", "arbitrary"),
        name="flash_attention",
    )(q, kv, kv)


def _out_ln_kernel(a_ref, w_ref, res_ref, g_ref, b_ref, o_ref):
    y = jnp.dot(a_ref[...], w_ref[...], preferred_element_type=F32)
    o_ref[...] = _deepnorm_ln(res_ref[...], y, g_ref[...], b_ref[...])


def _out_ln(a, w, res, g, b, tm=512):
    m, k = a.shape
    d = w.shape[1]
    tm = min(tm, m)
    return pl.pallas_call(
        _out_ln_kernel,
        grid=(m // tm,),
        in_specs=[pl.BlockSpec((tm, k), lambda i: (i, 0)),
                  _resident((k, d), lambda i: (0, 0)),
                  pl.BlockSpec((tm, d), lambda i: (i, 0)),
                  _resident((1, d), lambda i: (0, 0)),
                  _resident((1, d), lambda i: (0, 0))],
        out_specs=pl.BlockSpec((tm, d), lambda i: (i, 0)),
        out_shape=jax.ShapeDtypeStruct((m, d), F32),
        compiler_params=_cparams("parallel"),
        name="attn_out_ln",
    )(a, w, res, g.reshape(1, d), b.reshape(1, d))


def _ml_qkv_epilogue(acc, extra, o_ref, *, k_scale):
    scale = jnp.where(pl.program_id(1) == 1, k_scale, 1.0).astype(F32)
    o_ref[...] = (acc * scale).astype(o_ref.dtype)


def _log_sigmoid(x):
    return jnp.minimum(x, 0.0) - jnp.log(1.0 + jnp.exp(-jnp.abs(x)))


def _mlstm_kernel(q_ref, k_ref, v_ref, g_ref, bias_ref, o_ref, c_ref, n_ref, m_ref, *, dqk, dv):
    d = pl.program_id(1)
    chunk = q_ref.shape[0]

    @pl.when(pl.program_id(2) == 0)
    def _():
        c_ref[...] = jnp.zeros_like(c_ref)
        n_ref[...] = jnp.zeros_like(n_ref)
        m_ref[...] = jnp.zeros_like(m_ref)

    gates = g_ref[0, 0] + bias_ref[0]
    row = lax.broadcasted_iota(jnp.int32, (chunk, chunk), 0)
    col = lax.broadcasted_iota(jnp.int32, (chunk, chunk), 1)
    sgn = 1 - 2 * d
    mask = (row - col) * sgn >= 0
    eye = row == col

    for h in range(ML_HEADS):
        i_row = gates[h:h + 1, :]
        f_row = _log_sigmoid(gates[ML_HEADS + h:ML_HEADS + h + 1, :])
        m_prev = m_ref[h:h + 1, 0:1]
        n_prev = n_ref[h:h + 1, :]
        ct = c_ref[h]

        b_col = jnp.sum(jnp.where(mask, f_row, 0.0), axis=1, keepdims=True)
        b_row = jnp.sum(jnp.where(eye, b_col, 0.0), axis=0, keepdims=True)
        i_col = jnp.sum(jnp.where(eye, i_row, 0.0), axis=1, keepdims=True)
        a_row = i_row - b_row
        a_mat = jnp.where(mask, a_row, -jnp.inf)
        m_rows = jnp.maximum(m_prev, jnp.max(a_mat, axis=1, keepdims=True))
        w_mat = jnp.exp(a_mat - m_rows)
        g_col = jnp.exp(m_prev - m_rows)

        qh = q_ref[:, h * dqk:(h + 1) * dqk]
        kh = k_ref[:, h * dqk:(h + 1) * dqk]
        vh = v_ref[:, h * dv:(h + 1) * dv]
        s = lax.dot_general(qh, kh, (((1,), (1,)), ((), ())), preferred_element_type=F32) * w_mat
        inter = jnp.dot(qh, ct.astype(BF16), preferred_element_type=F32)
        num = g_col * inter + jnp.dot(s.astype(BF16), vh, preferred_element_type=F32)
        qn = jnp.sum(qh.astype(F32) * n_prev, axis=1, keepdims=True)
        den = g_col * qn + jnp.sum(s, axis=1, keepdims=True)
        o_ref[0, :, h * dv:(h + 1) * dv] = num / jnp.maximum(jnp.abs(den), jnp.exp(-(b_col + m_rows)))

        b_tot = jnp.sum(f_row, axis=1, keepdims=True)
        m_x = jnp.maximum(m_prev, jnp.max(a_row, axis=1, keepdims=True))
        g_s = jnp.exp(m_prev - m_x)
        ws_col = jnp.exp(i_col - b_col - m_x)
        ks = kh.astype(F32) * ws_col
        upd = lax.dot_general(ks.astype(BF16), vh, (((0,), (0,)), ((), ())),
                              preferred_element_type=F32)
        c_ref[h] = g_s * ct + upd
        n_ref[h:h + 1, :] = g_s * n_prev + jnp.sum(ks, axis=0, keepdims=True)
        m_ref[h:h + 1, :] = jnp.broadcast_to(b_tot + m_x, (1, m_ref.shape[1]))


def _mlstm(qkv, gates, bias, batch, seq):
    t = qkv.shape[0]
    dv = qkv.shape[1] // (2 * ML_HEADS)
    dqk = dv // 2
    chunk = min(ML_CHUNK, seq)
    nc = seq // chunk
    wqk = ML_HEADS * dqk
    wv = ML_HEADS * dv

    def rb(b, d, c):
        return b * nc + c + d * (nc - 1 - 2 * c)

    kern = functools.partial(_mlstm_kernel, dqk=dqk, dv=dv)
    return pl.pallas_call(
        kern,
        grid=(batch, 2, nc),
        in_specs=[pl.BlockSpec((chunk, wqk), lambda b, d, c: (rb(b, d, c), 0)),
                  pl.BlockSpec((chunk, wqk), lambda b, d, c: (rb(b, d, c), 1)),
                  pl.BlockSpec((chunk, wv), lambda b, d, c: (rb(b, d, c), 1)),
                  pl.BlockSpec((1, 1, 2 * ML_HEADS, chunk),
                               lambda b, d, c: (b, d, 0, c + d * (nc - 1 - 2 * c))),
                  pl.BlockSpec((1, 2 * ML_HEADS, 1), lambda b, d, c: (d, 0, 0))],
        out_specs=pl.BlockSpec((1, chunk, wv), lambda b, d, c: (d, rb(b, d, c), 0)),
        out_shape=jax.ShapeDtypeStruct((2, t, wv), F32),
        scratch_shapes=[pltpu.VMEM((ML_HEADS, dqk, dv), F32),
                        pltpu.VMEM((ML_HEADS, dqk), F32),
                        pltpu.VMEM((ML_HEADS, LANES), F32)],
        compiler_params=_cparams("parallel", "parallel", "arbitrary"),
        name="mlstm_chunks",
    )(qkv, qkv, qkv, gates, bias)


def _ml_out_kernel(hf_ref, hb_ref, og_ref, gain_ref, w_ref, res_ref, g_ref, b_ref, o_ref, a_ref, *, dv):
    for h in range(ML_HEADS):
        sl = slice(h * dv, (h + 1) * dv)
        hs = hf_ref[0, :, sl] + hb_ref[0, :, sl]
        ms = jnp.mean(hs * hs, axis=-1, keepdims=True)
        hn = hs * lax.rsqrt(ms + RMS_EPS) * gain_ref[:, sl]
        a_ref[:, sl] = (hn * jax.nn.sigmoid(og_ref[:, sl])).astype(BF16)
    y = jnp.dot(a_ref[...], w_ref[...], preferred_element_type=F32)
    o_ref[...] = _deepnorm_ln(res_ref[...], y, g_ref[...], b_ref[...])


def _ml_out(h2, og, gain, w, res, g, b, tm=256):
    _, m, d = h2.shape
    tm = min(tm, m)
    kern = functools.partial(_ml_out_kernel, dv=d // ML_HEADS)
    row = lambda i: (i, 0)
    const = lambda i: (0, 0)
    return pl.pallas_call(
        kern,
        grid=(m // tm,),
        in_specs=[pl.BlockSpec((1, tm, d), lambda i: (0, i, 0)),
                  pl.BlockSpec((1, tm, d), lambda i: (1, i, 0)),
                  pl.BlockSpec((tm, d), row),
                  _resident((1, d), const),
                  _resident((d, d), const),
                  pl.BlockSpec((tm, d), row),
                  _resident((1, d), const),
                  _resident((1, d), const)],
        out_specs=pl.BlockSpec((tm, d), row),
        out_shape=jax.ShapeDtypeStruct((m, d), F32),
        scratch_shapes=[pltpu.VMEM((tm, d), BF16)],
        compiler_params=_cparams("parallel"),
        name="mlstm_out_ln",
    )(h2, h2, og, gain.reshape(1, d), w, res, g.reshape(1, d), b.reshape(1, d))


def _xattn_kernel(x_ref, wq_ref, kv_ref, wo_ref, g_ref, b_ref, o_ref, a_ref):
    x = x_ref[...]
    d = x.shape[1]
    hd = d // XA_HEADS
    q = jnp.dot(x.astype(BF16), wq_ref[...], preferred_element_type=F32).astype(BF16)
    scale = hd ** -0.5
    for h in range(XA_HEADS):
        kh = kv_ref[:, h * hd:(h + 1) * hd]
        vh = kv_ref[:, d + h * hd:d + (h + 1) * hd]
        s = lax.dot_general(q[:, h * hd:(h + 1) * hd], kh, (((1,), (1,)), ((), ())),
                            preferred_element_type=F32) * scale
        e = jnp.exp(s - jnp.max(s, axis=-1, keepdims=True))
        p = e / jnp.sum(e, axis=-1, keepdims=True)
        a_ref[:, h * hd:(h + 1) * hd] = jnp.dot(p.astype(BF16), vh,
                                                preferred_element_type=F32).astype(BF16)
    y = jnp.dot(a_ref[...], wo_ref[...], preferred_element_type=F32)
    o_ref[...] = _deepnorm_ln(x, y, g_ref[...], b_ref[...])


def _xattn(x, wq, kv, wo, g, b, batch, seq, tq=512):
    t, d = x.shape
    mem_len = kv.shape[0] // batch
    tq = min(tq, seq)
    nq = seq // tq
    const = lambda bi, i: (0, 0)
    return pl.pallas_call(
        _xattn_kernel,
        grid=(batch, nq),
        in_specs=[pl.BlockSpec((tq, d), lambda bi, i: (bi * nq + i, 0)),
                  _resident((d, d), const),
                  pl.BlockSpec((mem_len, 2 * d), lambda bi, i: (bi, 0)),
                  _resident((d, d), const),
                  _resident((1, d), const),
                  _resident((1, d), const)],
        out_specs=pl.BlockSpec((tq, d), lambda bi, i: (bi * nq + i, 0)),
        out_shape=jax.ShapeDtypeStruct((t, d), F32),
        scratch_shapes=[pltpu.VMEM((tq, d), BF16)],
        compiler_params=_cparams("parallel", "arbitrary"),
        name="xattn_sublayer",
    )(x, wq, kv, wo, g.reshape(1, d), b.reshape(1, d))


def _mlp_kernel(x_ref, w1_ref, w2_ref, g_ref, b_ref, o_ref, xb_ref):
    f = pl.program_id(1)

    @pl.when(f == 0)
    def _():
        xb_ref[...] = x_ref[...].astype(BF16)
        o_ref[...] = jnp.zeros_like(o_ref)

    h = jnp.dot(xb_ref[...], w1_ref[...], preferred_element_type=F32)
    h = jnp.maximum(h, 0.0)
    o_ref[...] += jnp.dot((h * h).astype(BF16), w2_ref[...], preferred_element_type=F32)

    @pl.when(f == pl.num_programs(1) - 1)
    def _():
        o_ref[...] = _deepnorm_ln(x_ref[...], o_ref[...], g_ref[...], b_ref[...])


def _mlp(x, w1, w2, g, b, tm=1024, tf=512):
    m, d = x.shape
    dff = w1.shape[1]
    tm = min(tm, m)
    tf = min(tf, dff)
    const = lambda i, f: (0, 0)
    return pl.pallas_call(
        _mlp_kernel,
        grid=(m // tm, dff // tf),
        in_specs=[pl.BlockSpec((tm, d), lambda i, f: (i, 0)),
                  pl.BlockSpec((d, tf), lambda i, f: (0, f)),
                  pl.BlockSpec((tf, d), lambda i, f: (f, 0)),
                  _resident((1, d), const),
                  _resident((1, d), const)],
        out_specs=pl.BlockSpec((tm, d), lambda i, f: (i, 0)),
        out_shape=jax.ShapeDtypeStruct((m, d), F32),
        scratch_shapes=[pltpu.VMEM((tm, d), BF16)],
        compiler_params=_cparams("parallel", "arbitrary"),
        name="mlp_sublayer",
    )(x, w1, w2, g.reshape(1, d), b.reshape(1, d))


def _gqa_sublayer(x, batch, seq, w_in, q_gain, k_gain, w_out, ln_g, ln_b):
    d = x.shape[1]
    tables = _rope_tables(seq)
    nkv = d // HEAD_DIM // GQA_GROUP
    kd = nkv * HEAD_DIM
    w_q = _half_split_heads(w_in[:, :d]).astype(BF16)
    w_kv = jnp.concatenate([_half_split_heads(w_in[:, d:d + kd]), w_in[:, d + kd:]], axis=1).astype(BF16)
    q_scale = HEAD_DIM ** -0.5 * math.log2(math.e)
    q = _attn_proj(x, w_q, q_gain, tables, seq, min(1024, d) // HEAD_DIM, q_scale, "attn_q_proj")
    kv = _attn_proj(x, w_kv, k_gain, tables, seq, nkv, 1.0, "attn_kv_proj")
    o = _flash_attention(q, kv, batch, seq)
    return _out_ln(o, w_out.astype(BF16), x, ln_g, ln_b)


def _mlstm_sublayer(x, batch, seq, w_in, b_gate, head_gain, w_out, ln_g, ln_b):
    t, d = x.shape
    dv = d // ML_HEADS
    dqk = dv // 2
    nqk = ML_HEADS * dqk
    w_qkv = w_in[:, :2 * nqk + d].astype(BF16)
    w_o = w_in[:, 2 * nqk + d:2 * nqk + 2 * d].astype(BF16)
    ng = 4 * ML_HEADS
    w_g = jnp.pad(w_in[:, 2 * nqk + 2 * d:], ((0, 0), (0, LANES - ng))).astype(BF16)
    qkv = _proj(x, w_qkv, BF16, functools.partial(_ml_qkv_epilogue, k_scale=dqk ** -0.5), tn=nqk,
                name="mlstm_qkv_proj")
    og = _proj(x, w_o, F32, _plain_epilogue, name="mlstm_ogate_proj")
    gates = _proj(x, w_g, F32, _plain_epilogue, name="mlstm_gates_proj")
    gates = gates[:, :ng].reshape(batch, seq, 2, 2 * ML_HEADS).transpose(0, 2, 3, 1)
    bias = b_gate.astype(F32).reshape(2, 2 * ML_HEADS, 1)
    h2 = _mlstm(qkv, gates, bias, batch, seq)
    return _ml_out(h2, og, head_gain.reshape(-1), w_out.astype(BF16), x, ln_g, ln_b)


def _trunk(x3, mem3, p):
    batch, seq, d = x3.shape
    x = x3.reshape(batch * seq, d)
    mem = mem3.reshape(-1, d)
    for i in range(DEPTH):
        j = i // 2
        if i % 2 == 0:
            x = _gqa_sublayer(x, batch, seq, p['attn_w_in'][j], p['attn_q_gain'][j], p['attn_k_gain'][j],
                              p['attn_w_out'][j], p['ln_g'][i, 0], p['ln_b'][i, 0])
        else:
            x = _mlstm_sublayer(x, batch, seq, p['ml_w_in'][j], p['ml_b_gate'][j], p['ml_head_gain'][j],
                                p['ml_w_out'][j], p['ln_g'][i, 0], p['ln_b'][i, 0])
        kv = _proj(mem, p['xa_w_kv'][i].astype(BF16), BF16, _plain_epilogue, name="xattn_kv_proj")
        x = _xattn(x, p['xa_w_q'][i].astype(BF16), kv, p['xa_w_out'][i].astype(BF16),
                   p['ln_g'][i, 1], p['ln_b'][i, 1], batch, seq)
        x = _mlp(x, p['mlp_w1'][i].astype(BF16), p['mlp_w2'][i].astype(BF16),
                 p['ln_g'][i, 2], p['ln_b'][i, 2])
    return x.reshape(batch, seq, d)


def kernel(x_prompt, x_sample, mem_prompt, mem_sample, attn_w_in, attn_q_gain, attn_k_gain, attn_w_out,
           ml_w_in, ml_b_gate, ml_head_gain, ml_w_out, xa_w_q, xa_w_kv, xa_w_out, mlp_w1, mlp_w2,
           ln_g, ln_b):
    params = {
        'attn_w_in': attn_w_in, 'attn_q_gain': attn_q_gain, 'attn_k_gain': attn_k_gain,
        'attn_w_out': attn_w_out, 'ml_w_in': ml_w_in, 'ml_b_gate': ml_b_gate,
        'ml_head_gain': ml_head_gain, 'ml_w_out': ml_w_out, 'xa_w_q': xa_w_q, 'xa_w_kv': xa_w_kv,
        'xa_w_out': xa_w_out, 'mlp_w1': mlp_w1, 'mlp_w2': mlp_w2, 'ln_g': ln_g, 'ln_b': ln_b,
    }
    y_prompt = _trunk(x_prompt, mem_prompt, params)
    y_sample = _trunk(x_sample, mem_sample, params)
    return (y_prompt, y_sample)
```

```python
import functools
import math

import jax
import jax.numpy as jnp
from jax import lax
from jax.experimental import pallas as pl
from jax.experimental.pallas import tpu as pltpu

F32 = jnp.float32
BF16 = jnp.bfloat16

DEPTH = 4
HEAD_DIM = 128
GQA_GROUP = 4
GRID_W = 64
ROPE_THETA = 10000.0
ML_HEADS = 8
XA_HEADS = 4
DN_ALPHA = (2 * DEPTH) ** 0.25
LN_EPS = 1e-5
RMS_EPS = 1e-6

V7X_VMEM_BYTES = 64 * 1024 * 1024
VMEM_LIMIT_BYTES = V7X_VMEM_BYTES - 8 * 1024 * 1024
LANES = 128

ML_CHUNK = 128


def _cparams(*sem):
    return pltpu.CompilerParams(dimension_semantics=sem, vmem_limit_bytes=VMEM_LIMIT_BYTES)


def _resident(block_shape, index_map):
    return pl.BlockSpec(block_shape, index_map, pipeline_mode=pl.Buffered(1))


def _deepnorm_ln(res, y, g, b):
    z = DN_ALPHA * res + y
    mu = jnp.mean(z, axis=-1, keepdims=True)
    zc = z - mu
    var = jnp.mean(zc * zc, axis=-1, keepdims=True)
    return zc * lax.rsqrt(var + LN_EPS) * g + b


def _proj_kernel(x_ref, w_ref, *rest, n_extra, epilogue):
    extra = rest[:n_extra]
    o_ref = rest[n_extra]
    xb_ref = rest[n_extra + 1]

    @pl.when(pl.program_id(1) == 0)
    def _():
        xb_ref[...] = x_ref[...].astype(BF16)

    acc = jnp.dot(xb_ref[...], w_ref[...], preferred_element_type=F32)
    epilogue(acc, extra, o_ref)


def _proj(x, w, out_dtype, epilogue, extras=(), extra_specs=(), tm=1024, tn=1024, name="proj"):
    m, k = x.shape
    n = w.shape[1]
    tm = min(tm, m)
    tn = min(tn, n)
    kern = functools.partial(_proj_kernel, n_extra=len(extras), epilogue=epilogue)
    return pl.pallas_call(
        kern,
        grid=(m // tm, n // tn),
        in_specs=[pl.BlockSpec((tm, k), lambda i, j: (i, 0)),
                  pl.BlockSpec((k, tn), lambda i, j: (0, j))] + list(extra_specs),
        out_specs=pl.BlockSpec((tm, tn), lambda i, j: (i, j)),
        out_shape=jax.ShapeDtypeStruct((m, n), out_dtype),
        scratch_shapes=[pltpu.VMEM((tm, k), BF16)],
        compiler_params=_cparams("parallel", "arbitrary"),
        name=name,
    )(x, w, *extras)


def _plain_epilogue(acc, extra, o_ref):
    o_ref[...] = acc.astype(o_ref.dtype)


def _rope_tables(s):
    rows = s // GRID_W
    row_ids = jnp.repeat(jnp.arange(rows), GRID_W).astype(F32)
    col_ids = jnp.tile(jnp.arange(GRID_W), rows).astype(F32)
    axis_dim = HEAD_DIM // 2
    inv_freq = ROPE_THETA ** (-jnp.arange(0, axis_dim, 2, dtype=F32) / axis_dim)
    ang = jnp.concatenate([row_ids[:, None] * inv_freq, col_ids[:, None] * inv_freq], axis=1)
    cos = jnp.concatenate([jnp.cos(ang), jnp.cos(ang)], axis=1)
    sin = jnp.concatenate([-jnp.sin(ang), jnp.sin(ang)], axis=1)
    return cos, sin


def _half_split_heads(a):
    lead = a.shape[:-1]
    a = a.reshape(*lead, -1, 2, 2, HEAD_DIM // 4)
    return jnp.swapaxes(a, -3, -2).reshape(*lead, -1)


def _attn_proj_kernel(x_ref, w_ref, cos_ref, sin_ref, g_ref, gsw_ref, o_ref, xb_ref, acc_ref, *,
                      rope_heads, out_scale, rows):
    @pl.when(pl.program_id(1) == 0)
    def _():
        xb_ref[...] = x_ref[...].astype(BF16)

    acc_ref[...] = jnp.dot(xb_ref[...], w_ref[...], preferred_element_type=F32)
    tm, tn = acc_ref.shape
    g = g_ref[...]
    gsw = gsw_ref[...]

    ones = jnp.ones((HEAD_DIM, HEAD_DIM), BF16)

    def body(r, carry):
        r0 = pl.multiple_of(r * rows, rows)
        gc = cos_ref[pl.ds(r0, rows), :] * g
        gs = sin_ref[pl.ds(r0, rows), :] * gsw
        for h in range(rope_heads):
            sl = slice(h * HEAD_DIM, (h + 1) * HEAD_DIM)
            a = acc_ref[pl.ds(r0, rows), sl]
            sq = a * a
            sq_hi = sq.astype(BF16)
            sq_lo = (sq - sq_hi.astype(F32)).astype(BF16)
            ms = (jnp.dot(sq_hi, ones, preferred_element_type=F32)
                  + jnp.dot(sq_lo, ones, preferred_element_type=F32)) * (1.0 / HEAD_DIM)
            rr = lax.rsqrt(ms + RMS_EPS) * out_scale
            y = (a * gc + pltpu.roll(a, HEAD_DIM // 2, 1) * gs) * rr
            o_ref[pl.ds(r0, rows), sl] = y.astype(o_ref.dtype)
        return carry

    lax.fori_loop(0, tm // rows, body, 0, unroll=2)
    k_cols = rope_heads * HEAD_DIM
    for h in range((tn - k_cols) // HEAD_DIM):
        src = slice(k_cols + h * HEAD_DIM, k_cols + (h + 1) * HEAD_DIM)
        dst = k_cols + 2 * h * HEAD_DIM
        o_ref[:, dst:dst + HEAD_DIM] = acc_ref[:, src].astype(o_ref.dtype)
        o_ref[:, dst + HEAD_DIM:dst + 2 * HEAD_DIM] = jnp.ones((tm, HEAD_DIM), o_ref.dtype)


def _attn_proj(x, w, gain, tables, seq, rope_heads_per_block, out_scale, name, tm=1024, tn=1024, rows=128):
    m, k = x.shape
    n = w.shape[1]
    tm = min(tm, seq)
    tn = min(tn, n)
    rows = min(rows, tm)
    nblk = seq // tm
    t_out = rope_heads_per_block * HEAD_DIM + 2 * (tn - rope_heads_per_block * HEAD_DIM)
    cos, sin = tables
    gain = _half_split_heads(gain).reshape(1, HEAD_DIM)
    gain_sw = jnp.roll(gain, HEAD_DIM // 2, axis=1)
    kern = functools.partial(_attn_proj_kernel, rope_heads=rope_heads_per_block, out_scale=out_scale,
                             rows=rows)
    tab_spec = pl.BlockSpec((tm, HEAD_DIM), lambda i, j: (i % nblk, 0))
    g_spec = pl.BlockSpec((1, HEAD_DIM), lambda i, j: (0, 0))
    return pl.pallas_call(
        kern,
        grid=(m // tm, n // tn),
        in_specs=[pl.BlockSpec((tm, k), lambda i, j: (i, 0)),
                  pl.BlockSpec((k, tn), lambda i, j: (0, j)),
                  tab_spec, tab_spec, g_spec, g_spec],
        out_specs=pl.BlockSpec((tm, t_out), lambda i, j: (i, j)),
        out_shape=jax.ShapeDtypeStruct((m, (n // tn) * t_out), BF16),
        scratch_shapes=[pltpu.VMEM((tm, k), BF16), pltpu.VMEM((tm, tn), F32)],
        compiler_params=_cparams("parallel", "arbitrary"),
        name=name,
    )(x, w, cos, sin, gain, gain_sw)


def _flash_kernel(q_ref, k_ref, v_ref, o_ref, q4_ref, *, tq, tk):
    seq = k_ref.shape[0]
    for h in range(GQA_GROUP):
        q4_ref[h * tq:(h + 1) * tq, :] = q_ref[:, h * HEAD_DIM:(h + 1) * HEAD_DIM]
    q4 = q4_ref[...]
    rows = GQA_GROUP * tq

    m = jnp.full((rows, 1), -jnp.inf, F32)
    acc = jnp.zeros((rows, 2 * HEAD_DIM), F32)
    for c in range(seq // tk):
        kc = k_ref[c * tk:(c + 1) * tk, :]
        vc = v_ref[c * tk:(c + 1) * tk, :]
        s = lax.dot_general(q4, kc, (((1,), (1,)), ((), ())), preferred_element_type=F32)
        m_new = jnp.maximum(m, jnp.max(s, axis=-1, keepdims=True))
        p = jnp.exp2(s - m_new).astype(BF16)
        acc = jnp.exp2(m - m_new) * acc + jnp.dot(p, vc, preferred_element_type=F32)
        m = m_new
    out = acc[:, :HEAD_DIM] / acc[:, HEAD_DIM:]
    for h in range(GQA_GROUP):
        o_ref[:, h * HEAD_DIM:(h + 1) * HEAD_DIM] = out[h * tq:(h + 1) * tq, :].astype(o_ref.dtype)


def _flash_attention(q, kv, batch, seq, tq=512, tk=256):
    t, dq = q.shape
    nkv = dq // HEAD_DIM // GQA_GROUP
    tq = min(tq, seq)
    tk = min(tk, seq)
    nq = seq // tq
    gw = GQA_GROUP * HEAD_DIM
    v0 = nkv // 2
    kern = functools.partial(_flash_kernel, tq=tq, tk=tk)
    return pl.pallas_call(
        kern,
        grid=(batch, nkv, nq),
        in_specs=[pl.BlockSpec((tq, gw), lambda b, g, i: (b * nq + i, g)),
                  pl.BlockSpec((seq, HEAD_DIM), lambda b, g, i: (b, g)),
                  pl.BlockSpec((seq, 2 * HEAD_DIM), lambda b, g, i: (b, v0 + g))],
        out_specs=pl.BlockSpec((tq, gw), lambda b, g, i: (b * nq + i, g)),
        out_shape=jax.ShapeDtypeStruct((t, dq), BF16),
        scratch_shapes=[pltpu.VMEM((GQA_GROUP * tq, HEAD_DIM), BF16)],
        compiler_params=_cparams("parallel", "parallel", "arbitrary"),
        name="flash_attention",
    )(q, kv, kv)


def _out_ln_kernel(a_ref, w_ref, res_ref, g_ref, b_ref, o_ref):
    y = jnp.dot(a_ref[...], w_ref[...], preferred_element_type=F32)
    o_ref[...] = _deepnorm_ln(res_ref[...], y, g_ref[...], b_ref[...])


def _out_ln(a, w, res, g, b, tm=512):
    m, k = a.shape
    d = w.shape[1]
    tm = min(tm, m)
    return pl.pallas_call(
        _out_ln_kernel,
        grid=(m // tm,),
        in_specs=[pl.BlockSpec((tm, k), lambda i: (i, 0)),
                  _resident((k, d), lambda i: (0, 0)),
                  pl.BlockSpec((tm, d), lambda i: (i, 0)),
                  _resident((1, d), lambda i: (0, 0)),
                  _resident((1, d), lambda i: (0, 0))],
        out_specs=pl.BlockSpec((tm, d), lambda i: (i, 0)),
        out_shape=jax.ShapeDtypeStruct((m, d), F32),
        compiler_params=_cparams("parallel"),
        name="attn_out_ln",
    )(a, w, res, g.reshape(1, d), b.reshape(1, d))


def _ml_qkv_epilogue(acc, extra, o_ref, *, k_scale):
    scale = jnp.where(pl.program_id(1) == 1, k_scale, 1.0).astype(F32)
    o_ref[...] = (acc * scale).astype(o_ref.dtype)


def _log_sigmoid(x):
    return jnp.minimum(x, 0.0) - jnp.log(1.0 + jnp.exp(-jnp.abs(x)))


def _lane_scan(x, op, fill, backward):
    length = x.shape[1]
    lane = lax.broadcasted_iota(jnp.int32, x.shape, 1)
    fwd = bwd = x
    k = 1
    while k < length:
        fwd = op(fwd, jnp.where(lane >= k, pltpu.roll(fwd, k, 1), fill))
        bwd = op(bwd, jnp.where(lane < length - k, pltpu.roll(bwd, length - k, 1), fill))
        k *= 2
    return jnp.where(backward, bwd, fwd)


def _gate_prework(gates, backward):
    nh = ML_HEADS
    length = gates.shape[1]
    f_rows = _log_sigmoid(gates[nh:])
    b_rows = _lane_scan(f_rows, jnp.add, 0.0, backward)
    a_rows = gates[:nh] - b_rows
    cm_rows = _lane_scan(a_rows, jnp.maximum, -jnp.inf, backward)
    b_tot = jnp.broadcast_to(jnp.sum(f_rows, axis=1, keepdims=True), (nh, length))
    a_max = jnp.broadcast_to(jnp.max(a_rows, axis=1, keepdims=True), (nh, length))
    rows = jnp.concatenate([a_rows, b_tot, a_max, jnp.zeros((nh, length), F32)], axis=0)
    cols = jnp.concatenate([cm_rows, b_rows, jnp.zeros((length - 2 * nh, length), F32)], axis=0).T
    return rows, cols


def _mlstm_kernel(q_ref, kt_ref, v_ref, g_ref, gn_ref, bias_ref, o_ref, c_ref, m_ref, rows_ref, cols_ref,
                  *, dqk, dv):
    d = pl.program_id(1)
    chunk = q_ref.shape[0]
    nh = ML_HEADS
    backward = d == 1

    @pl.when(pl.program_id(2) == 0)
    def _():
        c_ref[...] = jnp.zeros_like(c_ref)
        m_ref[...] = jnp.zeros_like(m_ref)
        rows0, cols0 = _gate_prework(g_ref[0, 0] + bias_ref[0], backward)
        rows_ref[...] = rows0
        cols_ref[...] = cols0

    rows = rows_ref[...]
    cols = cols_ref[...]
    rows_next, cols_next = _gate_prework(gn_ref[0, 0] + bias_ref[0], backward)
    rows_ref[...] = rows_next
    cols_ref[...] = cols_next

    row = lax.broadcasted_iota(jnp.int32, (chunk, chunk), 0)
    col = lax.broadcasted_iota(jnp.int32, (chunk, chunk), 1)
    mask = (row - col) * (1 - 2 * d) >= 0

    a_rows = rows[:nh]
    m_prev_all = m_ref[:, 0:1]
    m_x_all = jnp.maximum(m_prev_all, rows[2 * nh:3 * nh, 0:1])
    g_s_all = jnp.exp(m_prev_all - m_x_all)
    ws_rows = jnp.exp(a_rows - m_x_all)
    m_ref[...] = jnp.broadcast_to(rows[nh:2 * nh, 0:1] + m_x_all, m_ref.shape)
    ones = jnp.ones((chunk, LANES), BF16)

    for h in range(nh):
        m_prev = m_prev_all[h:h + 1, :]
        cm_col = jnp.broadcast_to(cols[:, h:h + 1], (chunk, LANES))
        b_col = jnp.broadcast_to(cols[:, nh + h:nh + h + 1], (chunk, LANES))
        m_col = jnp.maximum(m_prev, cm_col)
        floor = jnp.exp(-(b_col + m_col))
        w_mat = jnp.exp(jnp.where(mask, a_rows[h:h + 1, :], -jnp.inf) - m_col)
        g_col = jnp.exp(m_prev - m_col)

        qh = q_ref[:, h * dqk:(h + 1) * dqk]
        kth = kt_ref[h * dqk:(h + 1) * dqk, :]
        v_ext = jnp.concatenate([v_ref[:, h * dv:(h + 1) * dv], ones], axis=1)
        ct = c_ref[h]
        s = jnp.dot(qh, kth, preferred_element_type=F32) * w_mat
        inter = jnp.dot(qh, ct.astype(BF16), preferred_element_type=F32)
        intra = jnp.dot(s.astype(BF16), v_ext, preferred_element_type=F32)
        den = jnp.maximum(jnp.abs(g_col * inter[:, dv:] + intra[:, dv:]), floor)
        for part in range(dv // LANES):
            sl = slice(part * LANES, (part + 1) * LANES)
            o_ref[0, :, h * dv + part * LANES:h * dv + (part + 1) * LANES] = (
                (g_col * inter[:, sl] + intra[:, sl]) / den)

        kts = (kth.astype(F32) * ws_rows[h:h + 1, :]).astype(BF16)
        c_ref[h] = g_s_all[h:h + 1, :] * ct + jnp.dot(kts, v_ext, preferred_element_type=F32)


def _mlstm(qkv, kt, gates_rows, bias_rows, batch, seq):
    t = qkv.shape[0]
    dv = qkv.shape[1] // (2 * ML_HEADS)
    dqk = dv // 2
    chunk = ML_CHUNK
    assert chunk == LANES and seq % chunk == 0
    nc = seq // chunk
    wqk = ML_HEADS * dqk
    wv = ML_HEADS * dv

    def cb(d, c):
        return c + d * (nc - 1 - 2 * c)

    def rb(b, d, c):
        return b * nc + cb(d, c)

    kern = functools.partial(_mlstm_kernel, dqk=dqk, dv=dv)
    return pl.pallas_call(
        kern,
        grid=(batch, 2, nc),
        in_specs=[pl.BlockSpec((chunk, wqk), lambda b, d, c: (rb(b, d, c), 0)),
                  pl.BlockSpec((wqk, chunk), lambda b, d, c: (b, cb(d, c))),
                  pl.BlockSpec((chunk, wv), lambda b, d, c: (rb(b, d, c), 1)),
                  pl.BlockSpec((1, 1, 2 * ML_HEADS, chunk), lambda b, d, c: (b, d, 0, cb(d, c))),
                  pl.BlockSpec((1, 1, 2 * ML_HEADS, chunk),
                               lambda b, d, c: (b, d, 0, cb(d, jnp.minimum(c + 1, nc - 1)))),
                  pl.BlockSpec((1, 2 * ML_HEADS, 1), lambda b, d, c: (d, 0, 0))],
        out_specs=pl.BlockSpec((1, chunk, wv), lambda b, d, c: (d, rb(b, d, c), 0)),
        out_shape=jax.ShapeDtypeStruct((2, t, wv), F32),
        scratch_shapes=[pltpu.VMEM((ML_HEADS, dqk, dv + LANES), F32),
                        pltpu.VMEM((ML_HEADS, LANES), F32),
                        pltpu.VMEM((4 * ML_HEADS, chunk), F32),
                        pltpu.VMEM((chunk, chunk), F32)],
        compiler_params=_cparams("parallel", "parallel", "arbitrary"),
        name="mlstm_chunks",
    )(qkv, kt, qkv, gates_rows, gates_rows, bias_rows)


def _ml_out_kernel(hf_ref, hb_ref, og_ref, gain_ref, w_ref, res_ref, g_ref, b_ref, o_ref, a_ref, *, dv):
    for h in range(ML_HEADS):
        sl = slice(h * dv, (h + 1) * dv)
        hs = hf_ref[0, :, sl] + hb_ref[0, :, sl]
        ms = jnp.mean(hs * hs, axis=-1, keepdims=True)
        hn = hs * lax.rsqrt(ms + RMS_EPS) * gain_ref[:, sl]
        a_ref[:, sl] = (hn * jax.nn.sigmoid(og_ref[:, sl])).astype(BF16)
    y = jnp.dot(a_ref[...], w_ref[...], preferred_element_type=F32)
    o_ref[...] = _deepnorm_ln(res_ref[...], y, g_ref[...], b_ref[...])


def _ml_out(h2, og, gain, w, res, g, b, tm=256):
    _, m, d = h2.shape
    tm = min(tm, m)
    kern = functools.partial(_ml_out_kernel, dv=d // ML_HEADS)
    row = lambda i: (i, 0)
    const = lambda i: (0, 0)
    return pl.pallas_call(
        kern,
        grid=(m // tm,),
        in_specs=[pl.BlockSpec((1, tm, d), lambda i: (0, i, 0)),
                  pl.BlockSpec((1, tm, d), lambda i: (1, i, 0)),
                  pl.BlockSpec((tm, d), row),
                  _resident((1, d), const),
                  _resident((d, d), const),
                  pl.BlockSpec((tm, d), row),
                  _resident((1, d), const),
                  _resident((1, d), const)],
        out_specs=pl.BlockSpec((tm, d), row),
        out_shape=jax.ShapeDtypeStruct((m, d), F32),
        scratch_shapes=[pltpu.VMEM((tm, d), BF16)],
        compiler_params=_cparams("parallel"),
        name="mlstm_out_ln",
    )(h2, h2, og, gain.reshape(1, d), w, res, g.reshape(1, d), b.reshape(1, d))


def _xattn_kernel(x_ref, wq_ref, kv_ref, wo_ref, g_ref, b_ref, o_ref, a_ref):
    x = x_ref[...]
    d = x.shape[1]
    hd = d // XA_HEADS
    q = jnp.dot(x.astype(BF16), wq_ref[...], preferred_element_type=F32).astype(BF16)
    scale = hd ** -0.5
    for h in range(XA_HEADS):
        kh = kv_ref[:, h * hd:(h + 1) * hd]
        vh = kv_ref[:, d + h * hd:d + (h + 1) * hd]
        s = lax.dot_general(q[:, h * hd:(h + 1) * hd], kh, (((1,), (1,)), ((), ())),
                            preferred_element_type=F32) * scale
        e = jnp.exp(s - jnp.max(s, axis=-1, keepdims=True))
        p = e / jnp.sum(e, axis=-1, keepdims=True)
        a_ref[:, h * hd:(h + 1) * hd] = jnp.dot(p.astype(BF16), vh,
                                                preferred_element_type=F32).astype(BF16)
    y = jnp.dot(a_ref[...], wo_ref[...], preferred_element_type=F32)
    o_ref[...] = _deepnorm_ln(x, y, g_ref[...], b_ref[...])


def _xattn(x, wq, kv, wo, g, b, batch, seq, tq=512):
    t, d = x.shape
    mem_len = kv.shape[0] // batch
    tq = min(tq, seq)
    nq = seq // tq
    const = lambda bi, i: (0, 0)
    return pl.pallas_call(
        _xattn_kernel,
        grid=(batch, nq),
        in_specs=[pl.BlockSpec((tq, d), lambda bi, i: (bi * nq + i, 0)),
                  _resident((d, d), const),
                  pl.BlockSpec((mem_len, 2 * d), lambda bi, i: (bi, 0)),
                  _resident((d, d), const),
                  _resident((1, d), const),
                  _resident((1, d), const)],
        out_specs=pl.BlockSpec((tq, d), lambda bi, i: (bi * nq + i, 0)),
        out_shape=jax.ShapeDtypeStruct((t, d), F32),
        scratch_shapes=[pltpu.VMEM((tq, d), BF16)],
        compiler_params=_cparams("parallel", "arbitrary"),
        name="xattn_sublayer",
    )(x, wq, kv, wo, g.reshape(1, d), b.reshape(1, d))


def _mlp_kernel(x_ref, w1_ref, w2_ref, g_ref, b_ref, o_ref, xb_ref):
    f = pl.program_id(1)

    @pl.when(f == 0)
    def _():
        xb_ref[...] = x_ref[...].astype(BF16)
        o_ref[...] = jnp.zeros_like(o_ref)

    h = jnp.dot(xb_ref[...], w1_ref[...], preferred_element_type=F32)
    h = jnp.maximum(h, 0.0)
    o_ref[...] += jnp.dot((h * h).astype(BF16), w2_ref[...], preferred_element_type=F32)

    @pl.when(f == pl.num_programs(1) - 1)
    def _():
        o_ref[...] = _deepnorm_ln(x_ref[...], o_ref[...], g_ref[...], b_ref[...])


def _mlp(x, w1, w2, g, b, tm=1024, tf=512):
    m, d = x.shape
    dff = w1.shape[1]
    tm = min(tm, m)
    tf = min(tf, dff)
    const = lambda i, f: (0, 0)
    return pl.pallas_call(
        _mlp_kernel,
        grid=(m // tm, dff // tf),
        in_specs=[pl.BlockSpec((tm, d), lambda i, f: (i, 0)),
                  pl.BlockSpec((d, tf), lambda i, f: (0, f)),
                  pl.BlockSpec((tf, d), lambda i, f: (f, 0)),
                  _resident((1, d), const),
                  _resident((1, d), const)],
        out_specs=pl.BlockSpec((tm, d), lambda i, f: (i, 0)),
        out_shape=jax.ShapeDtypeStruct((m, d), F32),
        scratch_shapes=[pltpu.VMEM((tm, d), BF16)],
        compiler_params=_cparams("parallel", "arbitrary"),
        name="mlp_sublayer",
    )(x, w1, w2, g.reshape(1, d), b.reshape(1, d))


def _gqa_sublayer(x, batch, seq, w_in, q_gain, k_gain, w_out, ln_g, ln_b):
    d = x.shape[1]
    tables = _rope_tables(seq)
    nkv = d // HEAD_DIM // GQA_GROUP
    kd = nkv * HEAD_DIM
    w_q = _half_split_heads(w_in[:, :d]).astype(BF16)
    w_kv = jnp.concatenate([_half_split_heads(w_in[:, d:d + kd]), w_in[:, d + kd:]], axis=1).astype(BF16)
    q_scale = HEAD_DIM ** -0.5 * math.log2(math.e)
    q = _attn_proj(x, w_q, q_gain, tables, seq, min(1024, d) // HEAD_DIM, q_scale, "attn_q_proj")
    kv = _attn_proj(x, w_kv, k_gain, tables, seq, nkv, 1.0, "attn_kv_proj")
    o = _flash_attention(q, kv, batch, seq)
    return _out_ln(o, w_out.astype(BF16), x, ln_g, ln_b)


def _mlstm_sublayer(x, batch, seq, w_in, b_gate, head_gain, w_out, ln_g, ln_b):
    t, d = x.shape
    dv = d // ML_HEADS
    dqk = dv // 2
    nqk = ML_HEADS * dqk
    w_qkv = w_in[:, :2 * nqk + d].astype(BF16)
    w_o = w_in[:, 2 * nqk + d:2 * nqk + 2 * d].astype(BF16)
    ng = 4 * ML_HEADS
    w_g = jnp.pad(w_in[:, 2 * nqk + 2 * d:], ((0, 0), (0, LANES - ng))).astype(BF16)
    qkv = _proj(x, w_qkv, BF16, functools.partial(_ml_qkv_epilogue, k_scale=dqk ** -0.5), tn=nqk,
                name="mlstm_qkv_proj")
    og = _proj(x, w_o, F32, _plain_epilogue, name="mlstm_ogate_proj")
    gates = _proj(x, w_g, F32, _plain_epilogue, name="mlstm_gates_proj")
    gates = gates[:, :ng].reshape(batch, seq, 2, 2 * ML_HEADS).transpose(0, 2, 3, 1)
    kt = qkv[:, nqk:2 * nqk].reshape(batch, seq, nqk).transpose(0, 2, 1).reshape(batch * nqk, seq)
    bias = b_gate.astype(F32).reshape(2, 2 * ML_HEADS, 1)
    h2 = _mlstm(qkv, kt, gates, bias, batch, seq)
    return _ml_out(h2, og, head_gain.reshape(-1), w_out.astype(BF16), x, ln_g, ln_b)


def _trunk(x3, mem3, p):
    batch, seq, d = x3.shape
    x = x3.reshape(batch * seq, d)
    mem = mem3.reshape(-1, d)
    for i in range(DEPTH):
        j = i // 2
        if i % 2 == 0:
            x = _gqa_sublayer(x, batch, seq, p['attn_w_in'][j], p['attn_q_gain'][j], p['attn_k_gain'][j],
                              p['attn_w_out'][j], p['ln_g'][i, 0], p['ln_b'][i, 0])
        else:
            x = _mlstm_sublayer(x, batch, seq, p['ml_w_in'][j], p['ml_b_gate'][j], p['ml_head_gain'][j],
                                p['ml_w_out'][j], p['ln_g'][i, 0], p['ln_b'][i, 0])
        kv = _proj(mem, p['xa_w_kv'][i].astype(BF16), BF16, _plain_epilogue, name="xattn_kv_proj")
        x = _xattn(x, p['xa_w_q'][i].astype(BF16), kv, p['xa_w_out'][i].astype(BF16),
                   p['ln_g'][i, 1], p['ln_b'][i, 1], batch, seq)
        x = _mlp(x, p['mlp_w1'][i].astype(BF16), p['mlp_w2'][i].astype(BF16),
                 p['ln_g'][i, 2], p['ln_b'][i, 2])
    return x.reshape(batch, seq, d)


def kernel(x_prompt, x_sample, mem_prompt, mem_sample, attn_w_in, attn_q_gain, attn_k_gain, attn_w_out,
           ml_w_in, ml_b_gate, ml_head_gain, ml_w_out, xa_w_q, xa_w_kv, xa_w_out, mlp_w1, mlp_w2,
           ln_g, ln_b):
    params = {
        'attn_w_in': attn_w_in, 'attn_q_gain': attn_q_gain, 'attn_k_gain': attn_k_gain,
        'attn_w_out': attn_w_out, 'ml_w_in': ml_w_in, 'ml_b_gate': ml_b_gate,
        'ml_head_gain': ml_head_gain, 'ml_w_out': ml_w_out, 'xa_w_q': xa_w_q, 'xa_w_kv': xa_w_kv,
        'xa_w_out': xa_w_out, 'mlp_w1': mlp_w1, 'mlp_w2': mlp_w2, 'ln_g': ln_g, 'ln_b': ln_b,
    }
    y_prompt = _trunk(x_prompt, mem_prompt, params)
    y_sample = _trunk(x_sample, mem_sample, params)
    return (y_prompt, y_sample)
```

```python
import functools
import math

import jax
import jax.numpy as jnp
from jax import lax
from jax.experimental import pallas as pl
from jax.experimental.pallas import tpu as pltpu

F32 = jnp.float32
BF16 = jnp.bfloat16

DEPTH = 4
HEAD_DIM = 128
GQA_GROUP = 4
GRID_W = 64
ROPE_THETA = 10000.0
ML_HEADS = 8
XA_HEADS = 4
DN_ALPHA = (2 * DEPTH) ** 0.25
LN_EPS = 1e-5
RMS_EPS = 1e-6

V7X_VMEM_BYTES = 64 * 1024 * 1024
VMEM_LIMIT_BYTES = V7X_VMEM_BYTES - 8 * 1024 * 1024
LANES = 128

ML_CHUNK = 128


def _cparams(*sem):
    return pltpu.CompilerParams(dimension_semantics=sem, vmem_limit_bytes=VMEM_LIMIT_BYTES)


def _resident(block_shape, index_map):
    return pl.BlockSpec(block_shape, index_map, pipeline_mode=pl.Buffered(1))


def _deepnorm_ln(res, y, g, b):
    z = DN_ALPHA * res + y
    mu = jnp.mean(z, axis=-1, keepdims=True)
    zc = z - mu
    var = jnp.mean(zc * zc, axis=-1, keepdims=True)
    return zc * lax.rsqrt(var + LN_EPS) * g + b


def _proj_kernel(x_ref, w_ref, *rest, n_extra, epilogue):
    extra = rest[:n_extra]
    o_ref = rest[n_extra]
    xb_ref = rest[n_extra + 1]

    @pl.when(pl.program_id(1) == 0)
    def _():
        xb_ref[...] = x_ref[...].astype(BF16)

    acc = jnp.dot(xb_ref[...], w_ref[...], preferred_element_type=F32)
    epilogue(acc, extra, o_ref)


def _proj(x, w, out_dtype, epilogue, extras=(), extra_specs=(), tm=1024, tn=1024, name="proj"):
    m, k = x.shape
    n = w.shape[1]
    tm = min(tm, m)
    tn = min(tn, n)
    kern = functools.partial(_proj_kernel, n_extra=len(extras), epilogue=epilogue)
    return pl.pallas_call(
        kern,
        grid=(m // tm, n // tn),
        in_specs=[pl.BlockSpec((tm, k), lambda i, j: (i, 0)),
                  pl.BlockSpec((k, tn), lambda i, j: (0, j))] + list(extra_specs),
        out_specs=pl.BlockSpec((tm, tn), lambda i, j: (i, j)),
        out_shape=jax.ShapeDtypeStruct((m, n), out_dtype),
        scratch_shapes=[pltpu.VMEM((tm, k), BF16)],
        compiler_params=_cparams("parallel", "arbitrary"),
        name=name,
    )(x, w, *extras)


def _plain_epilogue(acc, extra, o_ref):
    o_ref[...] = acc.astype(o_ref.dtype)


def _rope_tables(s):
    rows = s // GRID_W
    row_ids = jnp.repeat(jnp.arange(rows), GRID_W).astype(F32)
    col_ids = jnp.tile(jnp.arange(GRID_W), rows).astype(F32)
    axis_dim = HEAD_DIM // 2
    inv_freq = ROPE_THETA ** (-jnp.arange(0, axis_dim, 2, dtype=F32) / axis_dim)
    ang = jnp.concatenate([row_ids[:, None] * inv_freq, col_ids[:, None] * inv_freq], axis=1)
    cos = jnp.concatenate([jnp.cos(ang), jnp.cos(ang)], axis=1)
    sin = jnp.concatenate([-jnp.sin(ang), jnp.sin(ang)], axis=1)
    return cos, sin


def _half_split_heads(a):
    lead = a.shape[:-1]
    a = a.reshape(*lead, -1, 2, 2, HEAD_DIM // 4)
    return jnp.swapaxes(a, -3, -2).reshape(*lead, -1)


def _attn_proj_kernel(x_ref, w_ref, cos_ref, sin_ref, gq_ref, gk_ref, q_ref, kv_ref, xb_ref, acc_ref, *,
                      q_scale, rows):
    j = pl.program_id(1)
    last = pl.num_programs(1) - 1

    @pl.when(j == 0)
    def _():
        xb_ref[...] = x_ref[...].astype(BF16)

    acc_ref[...] = jnp.dot(xb_ref[...], w_ref[...], preferred_element_type=F32)
    tm, tn = acc_ref.shape
    ones = jnp.ones((HEAD_DIM, HEAD_DIM), BF16)

    def norm_rope(o_ref, heads, gains_ref, out_scale):
        g = gains_ref[0:1, :]
        gsw = gains_ref[1:2, :]

        def body(r, carry):
            r0 = pl.multiple_of(r * rows, rows)
            gc = cos_ref[pl.ds(r0, rows), :] * g
            gs = sin_ref[pl.ds(r0, rows), :] * gsw
            for h in range(heads):
                sl = slice(h * HEAD_DIM, (h + 1) * HEAD_DIM)
                a = acc_ref[pl.ds(r0, rows), sl]
                sq = a * a
                sq_hi = sq.astype(BF16)
                sq_lo = (sq - sq_hi.astype(F32)).astype(BF16)
                ms = (jnp.dot(sq_hi, ones, preferred_element_type=F32)
                      + jnp.dot(sq_lo, ones, preferred_element_type=F32)) * (1.0 / HEAD_DIM)
                rr = lax.rsqrt(ms + RMS_EPS) * out_scale
                y = (a * gc + pltpu.roll(a, HEAD_DIM // 2, 1) * gs) * rr
                o_ref[pl.ds(r0, rows), sl] = y.astype(o_ref.dtype)
            return carry

        lax.fori_loop(0, tm // rows, body, 0, unroll=2)

    @pl.when(j < last)
    def _():
        norm_rope(q_ref, tn // HEAD_DIM, gq_ref, q_scale)

    @pl.when(j == last)
    def _():
        k_heads = tn // (2 * HEAD_DIM)
        norm_rope(kv_ref, k_heads, gk_ref, 1.0)
        k_cols = k_heads * HEAD_DIM
        for h in range(k_heads):
            src = slice(k_cols + h * HEAD_DIM, k_cols + (h + 1) * HEAD_DIM)
            dst = k_cols + 2 * h * HEAD_DIM
            kv_ref[:, dst:dst + HEAD_DIM] = acc_ref[:, src].astype(kv_ref.dtype)
            kv_ref[:, dst + HEAD_DIM:dst + 2 * HEAD_DIM] = jnp.ones((tm, HEAD_DIM), kv_ref.dtype)


def _attn_proj(x, w, q_gain, k_gain, tables, seq, q_scale, tm=1024, rows=128):
    m, k = x.shape
    tn = w.shape[1] - k
    tm = min(tm, seq)
    rows = min(rows, tm)
    nblk = seq // tm
    nq = k // tn
    cos, sin = tables

    def gains(gain):
        gain = _half_split_heads(gain).reshape(1, HEAD_DIM)
        return jnp.concatenate([gain, jnp.roll(gain, HEAD_DIM // 2, axis=1)], axis=0)

    kern = functools.partial(_attn_proj_kernel, q_scale=q_scale, rows=rows)
    tab_spec = pl.BlockSpec((tm, HEAD_DIM), lambda i, j: (i % nblk, 0))
    g_spec = pl.BlockSpec((2, HEAD_DIM), lambda i, j: (0, 0))
    return pl.pallas_call(
        kern,
        grid=(m // tm, nq + 1),
        in_specs=[pl.BlockSpec((tm, k), lambda i, j: (i, 0)),
                  pl.BlockSpec((k, tn), lambda i, j: (0, j)),
                  tab_spec, tab_spec, g_spec, g_spec],
        out_specs=[pl.BlockSpec((tm, tn), lambda i, j: (i, jnp.minimum(j, nq - 1))),
                   pl.BlockSpec((tm, 3 * tn // 2), lambda i, j: (i, 0))],
        out_shape=[jax.ShapeDtypeStruct((m, k), BF16),
                   jax.ShapeDtypeStruct((m, 3 * tn // 2), BF16)],
        scratch_shapes=[pltpu.VMEM((tm, k), BF16), pltpu.VMEM((tm, tn), F32)],
        compiler_params=_cparams("parallel", "arbitrary"),
        name="attn_qkv_proj",
    )(x, w, cos, sin, gains(q_gain), gains(k_gain))


def _flash_kernel(q_ref, k_ref, v_ref, o_ref, q4_ref, *, tq, tk):
    seq = k_ref.shape[0]
    for h in range(GQA_GROUP):
        q4_ref[h * tq:(h + 1) * tq, :] = q_ref[:, h * HEAD_DIM:(h + 1) * HEAD_DIM]
    q4 = q4_ref[...]
    rows = GQA_GROUP * tq

    m = jnp.full((rows, 1), -jnp.inf, F32)
    acc = jnp.zeros((rows, 2 * HEAD_DIM), F32)
    for c in range(seq // tk):
        kc = k_ref[c * tk:(c + 1) * tk, :]
        vc = v_ref[c * tk:(c + 1) * tk, :]
        s = lax.dot_general(q4, kc, (((1,), (1,)), ((), ())), preferred_element_type=F32)
        m_new = jnp.maximum(m, jnp.max(s, axis=-1, keepdims=True))
        p = jnp.exp2(s - m_new).astype(BF16)
        acc = jnp.exp2(m - m_new) * acc + jnp.dot(p, vc, preferred_element_type=F32)
        m = m_new
    out = acc[:, :HEAD_DIM] / acc[:, HEAD_DIM:]
    for h in range(GQA_GROUP):
        o_ref[:, h * HEAD_DIM:(h + 1) * HEAD_DIM] = out[h * tq:(h + 1) * tq, :].astype(o_ref.dtype)


def _flash_attention(q, kv, batch, seq, tq=512, tk=256):
    t, dq = q.shape
    nkv = dq // HEAD_DIM // GQA_GROUP
    tq = min(tq, seq)
    tk = min(tk, seq)
    nq = seq // tq
    gw = GQA_GROUP * HEAD_DIM
    v0 = nkv // 2
    kern = functools.partial(_flash_kernel, tq=tq, tk=tk)
    return pl.pallas_call(
        kern,
        grid=(batch, nkv, nq),
        in_specs=[pl.BlockSpec((tq, gw), lambda b, g, i: (b * nq + i, g)),
                  pl.BlockSpec((seq, HEAD_DIM), lambda b, g, i: (b, g)),
                  pl.BlockSpec((seq, 2 * HEAD_DIM), lambda b, g, i: (b, v0 + g))],
        out_specs=pl.BlockSpec((tq, gw), lambda b, g, i: (b * nq + i, g)),
        out_shape=jax.ShapeDtypeStruct((t, dq), BF16),
        scratch_shapes=[pltpu.VMEM((GQA_GROUP * tq, HEAD_DIM), BF16)],
        compiler_params=_cparams("parallel", "parallel", "arbitrary"),
        name="flash_attention",
    )(q, kv, kv)


def _out_ln_kernel(a_ref, w_ref, res_ref, g_ref, b_ref, o_ref):
    y = jnp.dot(a_ref[...], w_ref[...], preferred_element_type=F32)
    o_ref[...] = _deepnorm_ln(res_ref[...], y, g_ref[...], b_ref[...])


def _out_ln(a, w, res, g, b, tm=512):
    m, k = a.shape
    d = w.shape[1]
    tm = min(tm, m)
    return pl.pallas_call(
        _out_ln_kernel,
        grid=(m // tm,),
        in_specs=[pl.BlockSpec((tm, k), lambda i: (i, 0)),
                  _resident((k, d), lambda i: (0, 0)),
                  pl.BlockSpec((tm, d), lambda i: (i, 0)),
                  _resident((1, d), lambda i: (0, 0)),
                  _resident((1, d), lambda i: (0, 0))],
        out_specs=pl.BlockSpec((tm, d), lambda i: (i, 0)),
        out_shape=jax.ShapeDtypeStruct((m, d), F32),
        compiler_params=_cparams("parallel"),
        name="attn_out_ln",
    )(a, w, res, g.reshape(1, d), b.reshape(1, d))


def _ml_in_kernel(x_ref, w_ref, wg_ref, qkv_ref, gates_ref, xb_ref, *, k_scale):
    j = pl.program_id(1)

    @pl.when(j == 0)
    def _():
        xb_ref[...] = x_ref[...].astype(BF16)
        gates_ref[...] = jnp.dot(xb_ref[...], wg_ref[...], preferred_element_type=F32)

    acc = jnp.dot(xb_ref[...], w_ref[...], preferred_element_type=F32)
    scale = jnp.where(j == 1, k_scale, 1.0).astype(F32)
    qkv_ref[...] = (acc * scale).astype(qkv_ref.dtype)


def _ml_in_proj(x, w_qkv, w_g, k_scale, tn, tm=1024):
    m, k = x.shape
    n = w_qkv.shape[1]
    tm = min(tm, m)
    kern = functools.partial(_ml_in_kernel, k_scale=k_scale)
    return pl.pallas_call(
        kern,
        grid=(m // tm, n // tn),
        in_specs=[pl.BlockSpec((tm, k), lambda i, j: (i, 0)),
                  pl.BlockSpec((k, tn), lambda i, j: (0, j)),
                  _resident((k, LANES), lambda i, j: (0, 0))],
        out_specs=[pl.BlockSpec((tm, tn), lambda i, j: (i, j)),
                   pl.BlockSpec((tm, LANES), lambda i, j: (i, 0))],
        out_shape=[jax.ShapeDtypeStruct((m, n), BF16), jax.ShapeDtypeStruct((m, LANES), F32)],
        scratch_shapes=[pltpu.VMEM((tm, k), BF16)],
        compiler_params=_cparams("parallel", "arbitrary"),
        name="mlstm_in_proj",
    )(x, w_qkv, w_g)


def _log_sigmoid(x):
    return jnp.minimum(x, 0.0) - jnp.log(1.0 + jnp.exp(-jnp.abs(x)))


def _lane_scan(x, op, fill, backward):
    length = x.shape[1]
    lane = lax.broadcasted_iota(jnp.int32, x.shape, 1)
    fwd = bwd = x
    k = 1
    while k < length:
        fwd = op(fwd, jnp.where(lane >= k, pltpu.roll(fwd, k, 1), fill))
        bwd = op(bwd, jnp.where(lane < length - k, pltpu.roll(bwd, length - k, 1), fill))
        k *= 2
    return jnp.where(backward, bwd, fwd)


def _gate_prework(gates, backward):
    nh = ML_HEADS
    length = gates.shape[1]
    f_rows = _log_sigmoid(gates[nh:])
    b_rows = _lane_scan(f_rows, jnp.add, 0.0, backward)
    a_rows = gates[:nh] - b_rows
    cm_rows = _lane_scan(a_rows, jnp.maximum, -jnp.inf, backward)
    b_tot = jnp.broadcast_to(jnp.sum(f_rows, axis=1, keepdims=True), (nh, length))
    a_max = jnp.broadcast_to(jnp.max(a_rows, axis=1, keepdims=True), (nh, length))
    rows = jnp.concatenate([a_rows, b_tot, a_max, jnp.zeros((nh, length), F32)], axis=0)
    cols = jnp.concatenate([cm_rows, b_rows, jnp.zeros((length - 2 * nh, length), F32)], axis=0).T
    return rows, cols


def _mlstm_kernel(q_ref, kt_ref, v_ref, g_ref, gn_ref, bias_ref, o_ref, c_ref, m_ref, rows_ref, cols_ref,
                  *, dqk, dv):
    d = pl.program_id(1)
    chunk = q_ref.shape[0]
    nh = ML_HEADS
    backward = d == 1

    @pl.when(pl.program_id(2) == 0)
    def _():
        c_ref[...] = jnp.zeros_like(c_ref)
        m_ref[...] = jnp.zeros_like(m_ref)
        rows0, cols0 = _gate_prework(g_ref[0, 0] + bias_ref[0], backward)
        rows_ref[...] = rows0
        cols_ref[...] = cols0

    rows = rows_ref[...]
    cols = cols_ref[...]
    rows_next, cols_next = _gate_prework(gn_ref[0, 0] + bias_ref[0], backward)
    rows_ref[...] = rows_next
    cols_ref[...] = cols_next

    row = lax.broadcasted_iota(jnp.int32, (chunk, chunk), 0)
    col = lax.broadcasted_iota(jnp.int32, (chunk, chunk), 1)
    mask = (row - col) * (1 - 2 * d) >= 0

    a_rows = rows[:nh]
    m_prev_all = m_ref[:, 0:1]
    m_x_all = jnp.maximum(m_prev_all, rows[2 * nh:3 * nh, 0:1])
    g_s_all = jnp.exp(m_prev_all - m_x_all)
    ws_rows = jnp.exp(a_rows - m_x_all)
    m_ref[...] = jnp.broadcast_to(rows[nh:2 * nh, 0:1] + m_x_all, m_ref.shape)
    ones = jnp.ones((chunk, LANES), BF16)

    for h in range(nh):
        m_prev = m_prev_all[h:h + 1, :]
        cm_col = jnp.broadcast_to(cols[:, h:h + 1], (chunk, LANES))
        b_col = jnp.broadcast_to(cols[:, nh + h:nh + h + 1], (chunk, LANES))
        m_col = jnp.maximum(m_prev, cm_col)
        floor = jnp.exp(-(b_col + m_col))
        w_mat = jnp.exp(jnp.where(mask, a_rows[h:h + 1, :], -jnp.inf) - m_col)
        g_col = jnp.exp(m_prev - m_col)

        qh = q_ref[:, h * dqk:(h + 1) * dqk]
        kth = kt_ref[h * dqk:(h + 1) * dqk, :]
        v_ext = jnp.concatenate([v_ref[:, h * dv:(h + 1) * dv], ones], axis=1)
        ct = c_ref[h]
        s = jnp.dot(qh, kth, preferred_element_type=F32) * w_mat
        inter = jnp.dot(qh, ct.astype(BF16), preferred_element_type=F32)
        intra = jnp.dot(s.astype(BF16), v_ext, preferred_element_type=F32)
        den = jnp.maximum(jnp.abs(g_col * inter[:, dv:] + intra[:, dv:]), floor)
        for part in range(dv // LANES):
            sl = slice(part * LANES, (part + 1) * LANES)
            o_ref[0, :, h * dv + part * LANES:h * dv + (part + 1) * LANES] = (
                (g_col * inter[:, sl] + intra[:, sl]) / den)

        kts = (kth.astype(F32) * ws_rows[h:h + 1, :]).astype(BF16)
        c_ref[h] = g_s_all[h:h + 1, :] * ct + jnp.dot(kts, v_ext, preferred_element_type=F32)


def _mlstm(qkv, kt, gates_rows, bias_rows, batch, seq):
    t = qkv.shape[0]
    dv = qkv.shape[1] // (2 * ML_HEADS)
    dqk = dv // 2
    chunk = ML_CHUNK
    assert chunk == LANES and seq % chunk == 0
    nc = seq // chunk
    wqk = ML_HEADS * dqk
    wv = ML_HEADS * dv

    def cb(d, c):
        return c + d * (nc - 1 - 2 * c)

    def rb(b, d, c):
        return b * nc + cb(d, c)

    kern = functools.partial(_mlstm_kernel, dqk=dqk, dv=dv)
    return pl.pallas_call(
        kern,
        grid=(batch, 2, nc),
        in_specs=[pl.BlockSpec((chunk, wqk), lambda b, d, c: (rb(b, d, c), 0)),
                  pl.BlockSpec((wqk, chunk), lambda b, d, c: (b, cb(d, c))),
                  pl.BlockSpec((chunk, wv), lambda b, d, c: (rb(b, d, c), 1)),
                  pl.BlockSpec((1, 1, 2 * ML_HEADS, chunk), lambda b, d, c: (b, d, 0, cb(d, c))),
                  pl.BlockSpec((1, 1, 2 * ML_HEADS, chunk),
                               lambda b, d, c: (b, d, 0, cb(d, jnp.minimum(c + 1, nc - 1)))),
                  pl.BlockSpec((1, 2 * ML_HEADS, 1), lambda b, d, c: (d, 0, 0))],
        out_specs=pl.BlockSpec((1, chunk, wv), lambda b, d, c: (d, rb(b, d, c), 0)),
        out_shape=jax.ShapeDtypeStruct((2, t, wv), F32),
        scratch_shapes=[pltpu.VMEM((ML_HEADS, dqk, dv + LANES), F32),
                        pltpu.VMEM((ML_HEADS, LANES), F32),
                        pltpu.VMEM((4 * ML_HEADS, chunk), F32),
                        pltpu.VMEM((chunk, chunk), F32)],
        compiler_params=_cparams("parallel", "parallel", "arbitrary"),
        name="mlstm_chunks",
    )(qkv, kt, qkv, gates_rows, gates_rows, bias_rows)


def _ml_out_kernel(hf_ref, hb_ref, x_ref, wo_ref, gain_ref, w_ref, g_ref, b_ref, o_ref, a_ref, *, dv):
    x = x_ref[...]
    og = jnp.dot(x.astype(BF16), wo_ref[...], preferred_element_type=F32)
    for h in range(ML_HEADS):
        sl = slice(h * dv, (h + 1) * dv)
        hs = hf_ref[0, :, sl] + hb_ref[0, :, sl]
        ms = jnp.mean(hs * hs, axis=-1, keepdims=True)
        hn = hs * lax.rsqrt(ms + RMS_EPS) * gain_ref[:, sl]
        a_ref[:, sl] = (hn * jax.nn.sigmoid(og[:, sl])).astype(BF16)
    y = jnp.dot(a_ref[...], w_ref[...], preferred_element_type=F32)
    o_ref[...] = _deepnorm_ln(x, y, g_ref[...], b_ref[...])


def _ml_out(h2, x, w_ogate, gain, w, g, b, tm=256):
    _, m, d = h2.shape
    tm = min(tm, m)
    kern = functools.partial(_ml_out_kernel, dv=d // ML_HEADS)
    row = lambda i: (i, 0)
    const = lambda i: (0, 0)
    return pl.pallas_call(
        kern,
        grid=(m // tm,),
        in_specs=[pl.BlockSpec((1, tm, d), lambda i: (0, i, 0)),
                  pl.BlockSpec((1, tm, d), lambda i: (1, i, 0)),
                  pl.BlockSpec((tm, d), row),
                  _resident((d, d), const),
                  _resident((1, d), const),
                  _resident((d, d), const),
                  _resident((1, d), const),
                  _resident((1, d), const)],
        out_specs=pl.BlockSpec((tm, d), row),
        out_shape=jax.ShapeDtypeStruct((m, d), F32),
        scratch_shapes=[pltpu.VMEM((tm, d), BF16)],
        compiler_params=_cparams("parallel"),
        name="mlstm_out_ln",
    )(h2, h2, x, w_ogate, gain.reshape(1, d), w, g.reshape(1, d), b.reshape(1, d))


def _xattn_kernel(x_ref, wq_ref, kv_ref, wo_ref, g_ref, b_ref, o_ref, a_ref):
    x = x_ref[...]
    d = x.shape[1]
    hd = d // XA_HEADS
    q = jnp.dot(x.astype(BF16), wq_ref[...], preferred_element_type=F32).astype(BF16)
    scale = hd ** -0.5
    for h in range(XA_HEADS):
        kh = kv_ref[:, h * hd:(h + 1) * hd]
        vh = kv_ref[:, d + h * hd:d + (h + 1) * hd]
        s = lax.dot_general(q[:, h * hd:(h + 1) * hd], kh, (((1,), (1,)), ((), ())),
                            preferred_element_type=F32) * scale
        e = jnp.exp(s - jnp.max(s, axis=-1, keepdims=True))
        p = e / jnp.sum(e, axis=-1, keepdims=True)
        a_ref[:, h * hd:(h + 1) * hd] = jnp.dot(p.astype(BF16), vh,
                                                preferred_element_type=F32).astype(BF16)
    y = jnp.dot(a_ref[...], wo_ref[...], preferred_element_type=F32)
    o_ref[...] = _deepnorm_ln(x, y, g_ref[...], b_ref[...])


def _xattn(x, wq, kv, wo, g, b, batch, seq, tq=512):
    t, d = x.shape
    mem_len = kv.shape[0] // batch
    tq = min(tq, seq)
    nq = seq // tq
    const = lambda bi, i: (0, 0)
    return pl.pallas_call(
        _xattn_kernel,
        grid=(batch, nq),
        in_specs=[pl.BlockSpec((tq, d), lambda bi, i: (bi * nq + i, 0)),
                  _resident((d, d), const),
                  pl.BlockSpec((mem_len, 2 * d), lambda bi, i: (bi, 0)),
                  _resident((d, d), const),
                  _resident((1, d), const),
                  _resident((1, d), const)],
        out_specs=pl.BlockSpec((tq, d), lambda bi, i: (bi * nq + i, 0)),
        out_shape=jax.ShapeDtypeStruct((t, d), F32),
        scratch_shapes=[pltpu.VMEM((tq, d), BF16)],
        compiler_params=_cparams("parallel", "arbitrary"),
        name="xattn_sublayer",
    )(x, wq, kv, wo, g.reshape(1, d), b.reshape(1, d))


def _mlp_kernel(x_ref, w1_ref, w2_ref, g_ref, b_ref, o_ref, xb_ref):
    f = pl.program_id(1)

    @pl.when(f == 0)
    def _():
        xb_ref[...] = x_ref[...].astype(BF16)
        o_ref[...] = jnp.zeros_like(o_ref)

    h = jnp.dot(xb_ref[...], w1_ref[...], preferred_element_type=F32)
    h = jnp.maximum(h, 0.0)
    o_ref[...] += jnp.dot((h * h).astype(BF16), w2_ref[...], preferred_element_type=F32)

    @pl.when(f == pl.num_programs(1) - 1)
    def _():
        o_ref[...] = _deepnorm_ln(x_ref[...], o_ref[...], g_ref[...], b_ref[...])


def _mlp(x, w1, w2, g, b, tm=1024, tf=512):
    m, d = x.shape
    dff = w1.shape[1]
    tm = min(tm, m)
    tf = min(tf, dff)
    const = lambda i, f: (0, 0)
    return pl.pallas_call(
        _mlp_kernel,
        grid=(m // tm, dff // tf),
        in_specs=[pl.BlockSpec((tm, d), lambda i, f: (i, 0)),
                  pl.BlockSpec((d, tf), lambda i, f: (0, f)),
                  pl.BlockSpec((tf, d), lambda i, f: (f, 0)),
                  _resident((1, d), const),
                  _resident((1, d), const)],
        out_specs=pl.BlockSpec((tm, d), lambda i, f: (i, 0)),
        out_shape=jax.ShapeDtypeStruct((m, d), F32),
        scratch_shapes=[pltpu.VMEM((tm, d), BF16)],
        compiler_params=_cparams("parallel", "arbitrary"),
        name="mlp_sublayer",
    )(x, w1, w2, g.reshape(1, d), b.reshape(1, d))


def _gqa_sublayer(x, batch, seq, w_in, q_gain, k_gain, w_out, ln_g, ln_b):
    d = x.shape[1]
    tables = _rope_tables(seq)
    nkv = d // HEAD_DIM // GQA_GROUP
    kd = nkv * HEAD_DIM
    w = jnp.concatenate([_half_split_heads(w_in[:, :d + kd]), w_in[:, d + kd:]], axis=1).astype(BF16)
    q_scale = HEAD_DIM ** -0.5 * math.log2(math.e)
    q, kv = _attn_proj(x, w, q_gain, k_gain, tables, seq, q_scale)
    o = _flash_attention(q, kv, batch, seq)
    return _out_ln(o, w_out.astype(BF16), x, ln_g, ln_b)


def _mlstm_sublayer(x, batch, seq, w_in, b_gate, head_gain, w_out, ln_g, ln_b):
    t, d = x.shape
    dv = d // ML_HEADS
    dqk = dv // 2
    nqk = ML_HEADS * dqk
    w_qkv = w_in[:, :2 * nqk + d].astype(BF16)
    w_o = w_in[:, 2 * nqk + d:2 * nqk + 2 * d].astype(BF16)
    ng = 4 * ML_HEADS
    w_g = jnp.pad(w_in[:, 2 * nqk + 2 * d:], ((0, 0), (0, LANES - ng))).astype(BF16)
    qkv, gates = _ml_in_proj(x, w_qkv, w_g, dqk ** -0.5, nqk)
    gates = gates[:, :ng].reshape(batch, seq, 2, 2 * ML_HEADS).transpose(0, 2, 3, 1)
    kt = qkv[:, nqk:2 * nqk].reshape(batch, seq, nqk).transpose(0, 2, 1).reshape(batch * nqk, seq)
    bias = b_gate.astype(F32).reshape(2, 2 * ML_HEADS, 1)
    h2 = _mlstm(qkv, kt, gates, bias, batch, seq)
    return _ml_out(h2, x, w_o, head_gain.reshape(-1), w_out.astype(BF16), ln_g, ln_b)


def _trunk(x3, mem3, p):
    batch, seq, d = x3.shape
    x = x3.reshape(batch * seq, d)
    mem = mem3.reshape(-1, d)
    for i in range(DEPTH):
        j = i // 2
        if i % 2 == 0:
            x = _gqa_sublayer(x, batch, seq, p['attn_w_in'][j], p['attn_q_gain'][j], p['attn_k_gain'][j],
                              p['attn_w_out'][j], p['ln_g'][i, 0], p['ln_b'][i, 0])
        else:
            x = _mlstm_sublayer(x, batch, seq, p['ml_w_in'][j], p['ml_b_gate'][j], p['ml_head_gain'][j],
                                p['ml_w_out'][j], p['ln_g'][i, 0], p['ln_b'][i, 0])
        kv = _proj(mem, p['xa_w_kv'][i].astype(BF16), BF16, _plain_epilogue, name="xattn_kv_proj")
        x = _xattn(x, p['xa_w_q'][i].astype(BF16), kv, p['xa_w_out'][i].astype(BF16),
                   p['ln_g'][i, 1], p['ln_b'][i, 1], batch, seq)
        x = _mlp(x, p['mlp_w1'][i].astype(BF16), p['mlp_w2'][i].astype(BF16),
                 p['ln_g'][i, 2], p['ln_b'][i, 2])
    return x.reshape(batch, seq, d)


def kernel(x_prompt, x_sample, mem_prompt, mem_sample, attn_w_in, attn_q_gain, attn_k_gain, attn_w_out,
           ml_w_in, ml_b_gate, ml_head_gain, ml_w_out, xa_w_q, xa_w_kv, xa_w_out, mlp_w1, mlp_w2,
           ln_g, ln_b):
    params = {
        'attn_w_in': attn_w_in, 'attn_q_gain': attn_q_gain, 'attn_k_gain': attn_k_gain,
        'attn_w_out': attn_w_out, 'ml_w_in': ml_w_in, 'ml_b_gate': ml_b_gate,
        'ml_head_gain': ml_head_gain, 'ml_w_out': ml_w_out, 'xa_w_q': xa_w_q, 'xa_w_kv': xa_w_kv,
        'xa_w_out': xa_w_out, 'mlp_w1': mlp_w1, 'mlp_w2': mlp_w2, 'ln_g': ln_g, 'ln_b': ln_b,
    }
    y_prompt = _trunk(x_prompt, mem_prompt, params)
    y_sample = _trunk(x_sample, mem_sample, params)
    return (y_prompt, y_sample)
```

```python
import functools
import math

import jax
import jax.numpy as jnp
from jax import lax
from jax.experimental import pallas as pl
from jax.experimental.pallas import tpu as pltpu

F32 = jnp.float32
BF16 = jnp.bfloat16

DEPTH = 4
HEAD_DIM = 128
GQA_GROUP = 4
GRID_W = 64
ROPE_THETA = 10000.0
ML_HEADS = 8
XA_HEADS = 4
DN_ALPHA = (2 * DEPTH) ** 0.25
LN_EPS = 1e-5
RMS_EPS = 1e-6

V7X_VMEM_BYTES = 64 * 1024 * 1024
VMEM_LIMIT_BYTES = V7X_VMEM_BYTES - 8 * 1024 * 1024
LANES = 128

ML_CHUNK = 128


def _cparams(*sem):
    return pltpu.CompilerParams(dimension_semantics=sem, vmem_limit_bytes=VMEM_LIMIT_BYTES)


def _resident(block_shape, index_map):
    return pl.BlockSpec(block_shape, index_map, pipeline_mode=pl.Buffered(1))


def _deepnorm_ln(res, y, g, b):
    z = DN_ALPHA * res + y
    mu = jnp.mean(z, axis=-1, keepdims=True)
    zc = z - mu
    var = jnp.mean(zc * zc, axis=-1, keepdims=True)
    return zc * lax.rsqrt(var + LN_EPS) * g + b


def _proj_kernel(x_ref, w_ref, *rest, n_extra, epilogue):
    extra = rest[:n_extra]
    o_ref = rest[n_extra]
    xb_ref = rest[n_extra + 1]

    @pl.when(pl.program_id(1) == 0)
    def _():
        xb_ref[...] = x_ref[...].astype(BF16)

    acc = jnp.dot(xb_ref[...], w_ref[...], preferred_element_type=F32)
    epilogue(acc, extra, o_ref)


def _proj(x, w, out_dtype, epilogue, extras=(), extra_specs=(), tm=1024, tn=1024, name="proj"):
    m, k = x.shape
    n = w.shape[1]
    tm = min(tm, m)
    tn = min(tn, n)
    kern = functools.partial(_proj_kernel, n_extra=len(extras), epilogue=epilogue)
    return pl.pallas_call(
        kern,
        grid=(m // tm, n // tn),
        in_specs=[pl.BlockSpec((tm, k), lambda i, j: (i, 0)),
                  pl.BlockSpec((k, tn), lambda i, j: (0, j))] + list(extra_specs),
        out_specs=pl.BlockSpec((tm, tn), lambda i, j: (i, j)),
        out_shape=jax.ShapeDtypeStruct((m, n), out_dtype),
        scratch_shapes=[pltpu.VMEM((tm, k), BF16)],
        compiler_params=_cparams("parallel", "arbitrary"),
        name=name,
    )(x, w, *extras)


def _plain_epilogue(acc, extra, o_ref):
    o_ref[...] = acc.astype(o_ref.dtype)


def _rope_tables(s):
    rows = s // GRID_W
    row_ids = jnp.repeat(jnp.arange(rows), GRID_W).astype(F32)
    col_ids = jnp.tile(jnp.arange(GRID_W), rows).astype(F32)
    axis_dim = HEAD_DIM // 2
    inv_freq = ROPE_THETA ** (-jnp.arange(0, axis_dim, 2, dtype=F32) / axis_dim)
    ang = jnp.concatenate([row_ids[:, None] * inv_freq, col_ids[:, None] * inv_freq], axis=1)
    cos = jnp.concatenate([jnp.cos(ang), jnp.cos(ang)], axis=1)
    sin = jnp.concatenate([-jnp.sin(ang), jnp.sin(ang)], axis=1)
    return cos, sin


def _half_split_heads(a):
    lead = a.shape[:-1]
    a = a.reshape(*lead, -1, 2, 2, HEAD_DIM // 4)
    return jnp.swapaxes(a, -3, -2).reshape(*lead, -1)


def _attn_proj_kernel(x_ref, w_ref, cos_ref, sin_ref, gq_ref, gk_ref, q_ref, kv_ref, xb_ref, acc_ref, *,
                      q_scale, rows):
    j = pl.program_id(1)
    last = pl.num_programs(1) - 1

    @pl.when(j == 0)
    def _():
        xb_ref[...] = x_ref[...].astype(BF16)

    acc_ref[...] = jnp.dot(xb_ref[...], w_ref[...], preferred_element_type=F32)
    tm, tn = acc_ref.shape
    ones = jnp.ones((HEAD_DIM, HEAD_DIM), BF16)

    def norm_rope(o_ref, heads, gains_ref, out_scale):
        g = gains_ref[0:1, :]
        gsw = gains_ref[1:2, :]

        def body(r, carry):
            r0 = pl.multiple_of(r * rows, rows)
            gc = cos_ref[pl.ds(r0, rows), :] * g
            gs = sin_ref[pl.ds(r0, rows), :] * gsw
            for h in range(heads):
                sl = slice(h * HEAD_DIM, (h + 1) * HEAD_DIM)
                a = acc_ref[pl.ds(r0, rows), sl]
                sq = a * a
                sq_hi = sq.astype(BF16)
                sq_lo = (sq - sq_hi.astype(F32)).astype(BF16)
                ssq = (jnp.dot(sq_hi, ones, preferred_element_type=F32)
                       + jnp.dot(sq_lo, ones, preferred_element_type=F32))
                rr = lax.rsqrt(ssq + HEAD_DIM * RMS_EPS) * (out_scale * HEAD_DIM ** 0.5)
                y = (a * gc + pltpu.roll(a, HEAD_DIM // 2, 1) * gs) * rr
                o_ref[pl.ds(r0, rows), sl] = y.astype(o_ref.dtype)
            return carry

        lax.fori_loop(0, tm // rows, body, 0, unroll=2)

    @pl.when(j < last)
    def _():
        norm_rope(q_ref, tn // HEAD_DIM, gq_ref, q_scale)

    @pl.when(j == last)
    def _():
        k_heads = tn // (2 * HEAD_DIM)
        norm_rope(kv_ref, k_heads, gk_ref, 1.0)
        k_cols = k_heads * HEAD_DIM
        for h in range(k_heads):
            src = slice(k_cols + h * HEAD_DIM, k_cols + (h + 1) * HEAD_DIM)
            dst = k_cols + 2 * h * HEAD_DIM
            kv_ref[:, dst:dst + HEAD_DIM] = acc_ref[:, src].astype(kv_ref.dtype)
            kv_ref[:, dst + HEAD_DIM:dst + 2 * HEAD_DIM] = jnp.ones((tm, HEAD_DIM), kv_ref.dtype)


def _attn_proj(x, w, q_gain, k_gain, tables, seq, q_scale, tm=1024, rows=128):
    m, k = x.shape
    tn = w.shape[1] - k
    tm = min(tm, seq)
    rows = min(rows, tm)
    nblk = seq // tm
    nq = k // tn
    cos, sin = tables

    def gains(gain):
        gain = _half_split_heads(gain).reshape(1, HEAD_DIM)
        return jnp.concatenate([gain, jnp.roll(gain, HEAD_DIM // 2, axis=1)], axis=0)

    kern = functools.partial(_attn_proj_kernel, q_scale=q_scale, rows=rows)
    tab_spec = pl.BlockSpec((tm, HEAD_DIM), lambda i, j: (i % nblk, 0))
    g_spec = pl.BlockSpec((2, HEAD_DIM), lambda i, j: (0, 0))
    return pl.pallas_call(
        kern,
        grid=(m // tm, nq + 1),
        in_specs=[pl.BlockSpec((tm, k), lambda i, j: (i, 0)),
                  pl.BlockSpec((k, tn), lambda i, j: (0, j)),
                  tab_spec, tab_spec, g_spec, g_spec],
        out_specs=[pl.BlockSpec((tm, tn), lambda i, j: (i, jnp.minimum(j, nq - 1))),
                   pl.BlockSpec((tm, 3 * tn // 2), lambda i, j: (i, 0))],
        out_shape=[jax.ShapeDtypeStruct((m, k), BF16),
                   jax.ShapeDtypeStruct((m, 3 * tn // 2), BF16)],
        scratch_shapes=[pltpu.VMEM((tm, k), BF16), pltpu.VMEM((tm, tn), F32)],
        compiler_params=_cparams("parallel", "arbitrary"),
        name="attn_qkv_proj",
    )(x, w, cos, sin, gains(q_gain), gains(k_gain))


def _flash_kernel(q_ref, k_ref, v_ref, o_ref, q4_ref, *, tq, tk):
    seq = k_ref.shape[0]
    for h in range(GQA_GROUP):
        q4_ref[h * tq:(h + 1) * tq, :] = q_ref[:, h * HEAD_DIM:(h + 1) * HEAD_DIM]
    q4 = q4_ref[...]
    rows = GQA_GROUP * tq

    m = jnp.full((rows, 1), -jnp.inf, F32)
    acc = jnp.zeros((rows, 2 * HEAD_DIM), F32)
    for c in range(seq // tk):
        kc = k_ref[c * tk:(c + 1) * tk, :]
        vc = v_ref[c * tk:(c + 1) * tk, :]
        s = lax.dot_general(q4, kc, (((1,), (1,)), ((), ())), preferred_element_type=F32)
        m_new = jnp.maximum(m, jnp.max(s, axis=-1, keepdims=True))
        p = jnp.exp2(s - m_new).astype(BF16)
        acc = jnp.exp2(m - m_new) * acc + jnp.dot(p, vc, preferred_element_type=F32)
        m = m_new
    out = acc[:, :HEAD_DIM] / acc[:, HEAD_DIM:]
    for h in range(GQA_GROUP):
        o_ref[:, h * HEAD_DIM:(h + 1) * HEAD_DIM] = out[h * tq:(h + 1) * tq, :].astype(o_ref.dtype)


def _flash_attention(q, kv, batch, seq, tq=512, tk=256):
    t, dq = q.shape
    nkv = dq // HEAD_DIM // GQA_GROUP
    tq = min(tq, seq)
    tk = min(tk, seq)
    nq = seq // tq
    gw = GQA_GROUP * HEAD_DIM
    v0 = nkv // 2
    kern = functools.partial(_flash_kernel, tq=tq, tk=tk)
    return pl.pallas_call(
        kern,
        grid=(batch, nkv, nq),
        in_specs=[pl.BlockSpec((tq, gw), lambda b, g, i: (b * nq + i, g)),
                  pl.BlockSpec((seq, HEAD_DIM), lambda b, g, i: (b, g)),
                  pl.BlockSpec((seq, 2 * HEAD_DIM), lambda b, g, i: (b, v0 + g))],
        out_specs=pl.BlockSpec((tq, gw), lambda b, g, i: (b * nq + i, g)),
        out_shape=jax.ShapeDtypeStruct((t, dq), BF16),
        scratch_shapes=[pltpu.VMEM((GQA_GROUP * tq, HEAD_DIM), BF16)],
        compiler_params=_cparams("parallel", "parallel", "arbitrary"),
        name="flash_attention",
    )(q, kv, kv)


def _out_ln_kernel(a_ref, w_ref, res_ref, g_ref, b_ref, o_ref):
    y = jnp.dot(a_ref[...], w_ref[...], preferred_element_type=F32)
    o_ref[...] = _deepnorm_ln(res_ref[...], y, g_ref[...], b_ref[...])


def _out_ln(a, w, res, g, b, tm=512):
    m, k = a.shape
    d = w.shape[1]
    tm = min(tm, m)
    return pl.pallas_call(
        _out_ln_kernel,
        grid=(m // tm,),
        in_specs=[pl.BlockSpec((tm, k), lambda i: (i, 0)),
                  _resident((k, d), lambda i: (0, 0)),
                  pl.BlockSpec((tm, d), lambda i: (i, 0)),
                  _resident((1, d), lambda i: (0, 0)),
                  _resident((1, d), lambda i: (0, 0))],
        out_specs=pl.BlockSpec((tm, d), lambda i: (i, 0)),
        out_shape=jax.ShapeDtypeStruct((m, d), F32),
        compiler_params=_cparams("parallel"),
        name="attn_out_ln",
    )(a, w, res, g.reshape(1, d), b.reshape(1, d))


def _ml_in_kernel(x_ref, w_ref, wg_ref, qkv_ref, gates_ref, xb_ref, *, k_scale):
    j = pl.program_id(1)

    @pl.when(j == 0)
    def _():
        xb_ref[...] = x_ref[...].astype(BF16)
        gates_ref[...] = jnp.dot(xb_ref[...], wg_ref[...], preferred_element_type=F32)

    acc = jnp.dot(xb_ref[...], w_ref[...], preferred_element_type=F32)
    scale = jnp.where(j == 1, k_scale, 1.0).astype(F32)
    qkv_ref[...] = (acc * scale).astype(qkv_ref.dtype)


def _ml_in_proj(x, w_qkv, w_g, k_scale, tn, tm=1024):
    m, k = x.shape
    n = w_qkv.shape[1]
    tm = min(tm, m)
    kern = functools.partial(_ml_in_kernel, k_scale=k_scale)
    return pl.pallas_call(
        kern,
        grid=(m // tm, n // tn),
        in_specs=[pl.BlockSpec((tm, k), lambda i, j: (i, 0)),
                  pl.BlockSpec((k, tn), lambda i, j: (0, j)),
                  _resident((k, LANES), lambda i, j: (0, 0))],
        out_specs=[pl.BlockSpec((tm, tn), lambda i, j: (i, j)),
                   pl.BlockSpec((tm, LANES), lambda i, j: (i, 0))],
        out_shape=[jax.ShapeDtypeStruct((m, n), BF16), jax.ShapeDtypeStruct((m, LANES), F32)],
        scratch_shapes=[pltpu.VMEM((tm, k), BF16)],
        compiler_params=_cparams("parallel", "arbitrary"),
        name="mlstm_in_proj",
    )(x, w_qkv, w_g)


def _log_sigmoid(x):
    return jnp.minimum(x, 0.0) - jnp.log(1.0 + jnp.exp(-jnp.abs(x)))


def _lane_scan(x, op, fill, backward):
    length = x.shape[1]
    lane = lax.broadcasted_iota(jnp.int32, x.shape, 1)
    k = 1
    while k < length:
        if backward:
            x = op(x, jnp.where(lane < length - k, pltpu.roll(x, length - k, 1), fill))
        else:
            x = op(x, jnp.where(lane >= k, pltpu.roll(x, k, 1), fill))
        k *= 2
    return x


def _gate_prework(gates, backward):
    nh = ML_HEADS
    length = gates.shape[1]
    f_rows = _log_sigmoid(gates[nh:])
    b_rows = _lane_scan(f_rows, jnp.add, 0.0, backward)
    a_rows = gates[:nh] - b_rows
    cm_rows = _lane_scan(a_rows, jnp.maximum, -jnp.inf, backward)
    b_tot = jnp.broadcast_to(jnp.sum(f_rows, axis=1, keepdims=True), (nh, length))
    a_max = jnp.broadcast_to(jnp.max(a_rows, axis=1, keepdims=True), (nh, length))
    rows = jnp.concatenate([a_rows, b_tot, a_max, jnp.zeros((nh, length), F32)], axis=0)
    cols = jnp.concatenate([cm_rows, b_rows, jnp.zeros((length - 2 * nh, length), F32)], axis=0).T
    return rows, cols


def _mlstm_direction(backward, q_ref, kt_ref, v_ref, gn_ref, bias, o_ref, c_ref, m_ref, rows_ref, cols_ref,
                     dqk, dv):
    chunk = q_ref.shape[0]
    nh = ML_HEADS
    rows = rows_ref[...]
    cols = cols_ref[...]
    rows_next, cols_next = _gate_prework(gn_ref[0, 0] + bias, backward)
    rows_ref[...] = rows_next
    cols_ref[...] = cols_next

    row = lax.broadcasted_iota(jnp.int32, (chunk, chunk), 0)
    col = lax.broadcasted_iota(jnp.int32, (chunk, chunk), 1)
    mask = (col >= row) if backward else (col <= row)

    a_rows = rows[:nh]
    m_prev_all = m_ref[:, 0:1]
    m_x_all = jnp.maximum(m_prev_all, rows[2 * nh:3 * nh, 0:1])
    g_s_all = jnp.exp(m_prev_all - m_x_all)
    ws_rows = jnp.exp(a_rows - m_x_all)
    m_ref[...] = jnp.broadcast_to(rows[nh:2 * nh, 0:1] + m_x_all, m_ref.shape)
    ones = jnp.ones((chunk, LANES), BF16)

    for h in range(nh):
        m_prev = m_prev_all[h:h + 1, :]
        cm_col = jnp.broadcast_to(cols[:, h:h + 1], (chunk, LANES))
        b_col = jnp.broadcast_to(cols[:, nh + h:nh + h + 1], (chunk, LANES))
        m_col = jnp.maximum(m_prev, cm_col)
        floor = jnp.exp(-(b_col + m_col))
        w_mat = jnp.exp(jnp.where(mask, a_rows[h:h + 1, :], -jnp.inf) - m_col)
        g_col = jnp.exp(m_prev - m_col)

        qh = q_ref[:, h * dqk:(h + 1) * dqk]
        kth = kt_ref[h * dqk:(h + 1) * dqk, :]
        v_ext = jnp.concatenate([v_ref[:, h * dv:(h + 1) * dv], ones], axis=1)
        ct = c_ref[h]
        s = jnp.dot(qh, kth, preferred_element_type=F32) * w_mat
        inter = jnp.dot(qh, ct.astype(BF16), preferred_element_type=F32)
        intra = jnp.dot(s.astype(BF16), v_ext, preferred_element_type=F32)
        den = jnp.maximum(jnp.abs(g_col * inter[:, dv:] + intra[:, dv:]), floor)
        for part in range(dv // LANES):
            sl = slice(part * LANES, (part + 1) * LANES)
            o_ref[:, h * dv + part * LANES:h * dv + (part + 1) * LANES] = (
                (g_col * inter[:, sl] + intra[:, sl]) / den)

        kts = (kth.astype(F32) * ws_rows[h:h + 1, :]).astype(BF16)
        c_ref[h] = g_s_all[h:h + 1, :] * ct + jnp.dot(kts, v_ext, preferred_element_type=F32)


def _mlstm_kernel(qf_ref, ktf_ref, vf_ref, gf_ref, gnf_ref, qb_ref, ktb_ref, vb_ref, gb_ref, gnb_ref, bias_ref,
                  of_ref, ob_ref, c_ref, m_ref, rows_ref, cols_ref, *, dqk, dv):
    @pl.when(pl.program_id(1) == 0)
    def _():
        c_ref[...] = jnp.zeros_like(c_ref)
        m_ref[...] = jnp.zeros_like(m_ref)
        for d, g_ref in enumerate((gf_ref, gb_ref)):
            rows0, cols0 = _gate_prework(g_ref[0, 0] + bias_ref[d], d == 1)
            rows_ref[d] = rows0
            cols_ref[d] = cols0

    dirs = ((qf_ref, ktf_ref, vf_ref, gnf_ref, of_ref), (qb_ref, ktb_ref, vb_ref, gnb_ref, ob_ref))
    for d, (q_ref, kt_ref, v_ref, gn_ref, o_ref) in enumerate(dirs):
        _mlstm_direction(d == 1, q_ref, kt_ref, v_ref, gn_ref, bias_ref[d], o_ref, c_ref.at[d], m_ref.at[d],
                         rows_ref.at[d], cols_ref.at[d], dqk, dv)


def _mlstm(qkv, kt, gates_rows, bias_rows, batch, seq):
    t = qkv.shape[0]
    dv = qkv.shape[1] // (2 * ML_HEADS)
    dqk = dv // 2
    chunk = ML_CHUNK
    assert chunk == LANES and seq % chunk == 0
    nc = seq // chunk
    wqk = ML_HEADS * dqk
    wv = ML_HEADS * dv

    def specs(d):
        def cb(c):
            return nc - 1 - c if d else c

        def cn(c):
            return cb(jnp.minimum(c + 1, nc - 1))

        return [pl.BlockSpec((chunk, wqk), lambda b, c: (b * nc + cb(c), 0)),
                pl.BlockSpec((wqk, chunk), lambda b, c: (b, cb(c))),
                pl.BlockSpec((chunk, wv), lambda b, c: (b * nc + cb(c), 1)),
                pl.BlockSpec((1, 1, 2 * ML_HEADS, chunk), lambda b, c: (b, d, 0, cb(c))),
                pl.BlockSpec((1, 1, 2 * ML_HEADS, chunk), lambda b, c: (b, d, 0, cn(c)))]

    kern = functools.partial(_mlstm_kernel, dqk=dqk, dv=dv)
    return pl.pallas_call(
        kern,
        grid=(batch, nc),
        in_specs=specs(0) + specs(1) + [pl.BlockSpec((2, 2 * ML_HEADS, 1), lambda b, c: (0, 0, 0))],
        out_specs=[pl.BlockSpec((chunk, wv), lambda b, c: (b * nc + c, 0)),
                   pl.BlockSpec((chunk, wv), lambda b, c: (b * nc + nc - 1 - c, 0))],
        out_shape=[jax.ShapeDtypeStruct((t, wv), F32), jax.ShapeDtypeStruct((t, wv), F32)],
        scratch_shapes=[pltpu.VMEM((2, ML_HEADS, dqk, dv + LANES), F32),
                        pltpu.VMEM((2, ML_HEADS, LANES), F32),
                        pltpu.VMEM((2, 4 * ML_HEADS, chunk), F32),
                        pltpu.VMEM((2, chunk, chunk), F32)],
        compiler_params=_cparams("parallel", "arbitrary"),
        name="mlstm_chunks",
    )(qkv, kt, qkv, gates_rows, gates_rows, qkv, kt, qkv, gates_rows, gates_rows, bias_rows)


def _ml_out_kernel(hf_ref, hb_ref, x_ref, wo_ref, gain_ref, w_ref, g_ref, b_ref, o_ref, a_ref, *, dv):
    x = x_ref[...]
    og = jnp.dot(x.astype(BF16), wo_ref[...], preferred_element_type=F32)
    for h in range(ML_HEADS):
        sl = slice(h * dv, (h + 1) * dv)
        hs = hf_ref[:, sl] + hb_ref[:, sl]
        ms = jnp.mean(hs * hs, axis=-1, keepdims=True)
        hn = hs * lax.rsqrt(ms + RMS_EPS) * gain_ref[:, sl]
        a_ref[:, sl] = (hn * jax.nn.sigmoid(og[:, sl])).astype(BF16)
    y = jnp.dot(a_ref[...], w_ref[...], preferred_element_type=F32)
    o_ref[...] = _deepnorm_ln(x, y, g_ref[...], b_ref[...])


def _ml_out(h_fwd, h_bwd, x, w_ogate, gain, w, g, b, tm=256):
    m, d = h_fwd.shape
    tm = min(tm, m)
    kern = functools.partial(_ml_out_kernel, dv=d // ML_HEADS)
    row = lambda i: (i, 0)
    const = lambda i: (0, 0)
    return pl.pallas_call(
        kern,
        grid=(m // tm,),
        in_specs=[pl.BlockSpec((tm, d), row),
                  pl.BlockSpec((tm, d), row),
                  pl.BlockSpec((tm, d), row),
                  _resident((d, d), const),
                  _resident((1, d), const),
                  _resident((d, d), const),
                  _resident((1, d), const),
                  _resident((1, d), const)],
        out_specs=pl.BlockSpec((tm, d), row),
        out_shape=jax.ShapeDtypeStruct((m, d), F32),
        scratch_shapes=[pltpu.VMEM((tm, d), BF16)],
        compiler_params=_cparams("parallel"),
        name="mlstm_out_ln",
    )(h_fwd, h_bwd, x, w_ogate, gain.reshape(1, d), w, g.reshape(1, d), b.reshape(1, d))


def _xattn_kernel(x_ref, wq_ref, kv_ref, wo_ref, g_ref, b_ref, o_ref, a_ref):
    x = x_ref[...]
    d = x.shape[1]
    hd = d // XA_HEADS
    q = jnp.dot(x.astype(BF16), wq_ref[...], preferred_element_type=F32).astype(BF16)
    scale = hd ** -0.5
    for h in range(XA_HEADS):
        kh = kv_ref[:, h * hd:(h + 1) * hd]
        vh = kv_ref[:, d + h * hd:d + (h + 1) * hd]
        s = lax.dot_general(q[:, h * hd:(h + 1) * hd], kh, (((1,), (1,)), ((), ())),
                            preferred_element_type=F32) * scale
        e = jnp.exp(s - jnp.max(s, axis=-1, keepdims=True))
        p = e / jnp.sum(e, axis=-1, keepdims=True)
        a_ref[:, h * hd:(h + 1) * hd] = jnp.dot(p.astype(BF16), vh,
                                                preferred_element_type=F32).astype(BF16)
    y = jnp.dot(a_ref[...], wo_ref[...], preferred_element_type=F32)
    o_ref[...] = _deepnorm_ln(x, y, g_ref[...], b_ref[...])


def _xattn(x, wq, kv, wo, g, b, batch, seq, tq=512):
    t, d = x.shape
    mem_len = kv.shape[0] // batch
    tq = min(tq, seq)
    nq = seq // tq
    const = lambda bi, i: (0, 0)
    return pl.pallas_call(
        _xattn_kernel,
        grid=(batch, nq),
        in_specs=[pl.BlockSpec((tq, d), lambda bi, i: (bi * nq + i, 0)),
                  _resident((d, d), const),
                  pl.BlockSpec((mem_len, 2 * d), lambda bi, i: (bi, 0)),
                  _resident((d, d), const),
                  _resident((1, d), const),
                  _resident((1, d), const)],
        out_specs=pl.BlockSpec((tq, d), lambda bi, i: (bi * nq + i, 0)),
        out_shape=jax.ShapeDtypeStruct((t, d), F32),
        scratch_shapes=[pltpu.VMEM((tq, d), BF16)],
        compiler_params=_cparams("parallel", "arbitrary"),
        name="xattn_sublayer",
    )(x, wq, kv, wo, g.reshape(1, d), b.reshape(1, d))


def _mlp_kernel(x_ref, w1_ref, w2_ref, g_ref, b_ref, o_ref, xb_ref):
    f = pl.program_id(1)

    @pl.when(f == 0)
    def _():
        xb_ref[...] = x_ref[...].astype(BF16)
        o_ref[...] = jnp.zeros_like(o_ref)

    h = jnp.dot(xb_ref[...], w1_ref[...], preferred_element_type=F32)
    h = jnp.maximum(h, 0.0)
    o_ref[...] += jnp.dot((h * h).astype(BF16), w2_ref[...], preferred_element_type=F32)

    @pl.when(f == pl.num_programs(1) - 1)
    def _():
        o_ref[...] = _deepnorm_ln(x_ref[...], o_ref[...], g_ref[...], b_ref[...])


def _mlp(x, w1, w2, g, b, tm=1024, tf=512):
    m, d = x.shape
    dff = w1.shape[1]
    tm = min(tm, m)
    tf = min(tf, dff)
    const = lambda i, f: (0, 0)
    return pl.pallas_call(
        _mlp_kernel,
        grid=(m // tm, dff // tf),
        in_specs=[pl.BlockSpec((tm, d), lambda i, f: (i, 0)),
                  pl.BlockSpec((d, tf), lambda i, f: (0, f)),
                  pl.BlockSpec((tf, d), lambda i, f: (f, 0)),
                  _resident((1, d), const),
                  _resident((1, d), const)],
        out_specs=pl.BlockSpec((tm, d), lambda i, f: (i, 0)),
        out_shape=jax.ShapeDtypeStruct((m, d), F32),
        scratch_shapes=[pltpu.VMEM((tm, d), BF16)],
        compiler_params=_cparams("parallel", "arbitrary"),
        name="mlp_sublayer",
    )(x, w1, w2, g.reshape(1, d), b.reshape(1, d))


def _gqa_sublayer(x, batch, seq, w_in, q_gain, k_gain, w_out, ln_g, ln_b):
    d = x.shape[1]
    tables = _rope_tables(seq)
    nkv = d // HEAD_DIM // GQA_GROUP
    kd = nkv * HEAD_DIM
    w = jnp.concatenate([_half_split_heads(w_in[:, :d + kd]), w_in[:, d + kd:]], axis=1).astype(BF16)
    q_scale = HEAD_DIM ** -0.5 * math.log2(math.e)
    q, kv = _attn_proj(x, w, q_gain, k_gain, tables, seq, q_scale)
    o = _flash_attention(q, kv, batch, seq)
    return _out_ln(o, w_out.astype(BF16), x, ln_g, ln_b)


def _mlstm_sublayer(x, batch, seq, w_in, b_gate, head_gain, w_out, ln_g, ln_b):
    t, d = x.shape
    dv = d // ML_HEADS
    dqk = dv // 2
    nqk = ML_HEADS * dqk
    w_qkv = w_in[:, :2 * nqk + d].astype(BF16)
    w_o = w_in[:, 2 * nqk + d:2 * nqk + 2 * d].astype(BF16)
    ng = 4 * ML_HEADS
    w_g = jnp.pad(w_in[:, 2 * nqk + 2 * d:], ((0, 0), (0, LANES - ng))).astype(BF16)
    qkv, gates = _ml_in_proj(x, w_qkv, w_g, dqk ** -0.5, nqk)
    gates = gates[:, :ng].reshape(batch, seq, 2, 2 * ML_HEADS).transpose(0, 2, 3, 1)
    kt = qkv[:, nqk:2 * nqk].reshape(batch, seq, nqk).transpose(0, 2, 1).reshape(batch * nqk, seq)
    bias = b_gate.astype(F32).reshape(2, 2 * ML_HEADS, 1)
    h_fwd, h_bwd = _mlstm(qkv, kt, gates, bias, batch, seq)
    return _ml_out(h_fwd, h_bwd, x, w_o, head_gain.reshape(-1), w_out.astype(BF16), ln_g, ln_b)


def _trunk(x3, mem3, p):
    batch, seq, d = x3.shape
    x = x3.reshape(batch * seq, d)
    mem = mem3.reshape(-1, d)
    for i in range(DEPTH):
        j = i // 2
        if i % 2 == 0:
            x = _gqa_sublayer(x, batch, seq, p['attn_w_in'][j], p['attn_q_gain'][j], p['attn_k_gain'][j],
                              p['attn_w_out'][j], p['ln_g'][i, 0], p['ln_b'][i, 0])
        else:
            x = _mlstm_sublayer(x, batch, seq, p['ml_w_in'][j], p['ml_b_gate'][j], p['ml_head_gain'][j],
                                p['ml_w_out'][j], p['ln_g'][i, 0], p['ln_b'][i, 0])
        kv = _proj(mem, p['xa_w_kv'][i].astype(BF16), BF16, _plain_epilogue, name="xattn_kv_proj")
        x = _xattn(x, p['xa_w_q'][i].astype(BF16), kv, p['xa_w_out'][i].astype(BF16),
                   p['ln_g'][i, 1], p['ln_b'][i, 1], batch, seq)
        x = _mlp(x, p['mlp_w1'][i].astype(BF16), p['mlp_w2'][i].astype(BF16),
                 p['ln_g'][i, 2], p['ln_b'][i, 2])
    return x.reshape(batch, seq, d)


def kernel(x_prompt, x_sample, mem_prompt, mem_sample, attn_w_in, attn_q_gain, attn_k_gain, attn_w_out,
           ml_w_in, ml_b_gate, ml_head_gain, ml_w_out, xa_w_q, xa_w_kv, xa_w_out, mlp_w1, mlp_w2,
           ln_g, ln_b):
    params = {
        'attn_w_in': attn_w_in, 'attn_q_gain': attn_q_gain, 'attn_k_gain': attn_k_gain,
        'attn_w_out': attn_w_out, 'ml_w_in': ml_w_in, 'ml_b_gate': ml_b_gate,
        'ml_head_gain': ml_head_gain, 'ml_w_out': ml_w_out, 'xa_w_q': xa_w_q, 'xa_w_kv': xa_w_kv,
        'xa_w_out': xa_w_out, 'mlp_w1': mlp_w1, 'mlp_w2': mlp_w2, 'ln_g': ln_g, 'ln_b': ln_b,
    }
    y_prompt = _trunk(x_prompt, mem_prompt, params)
    y_sample = _trunk(x_sample, mem_sample, params)
    return (y_prompt, y_sample)
```

```python
import functools
import math

import jax
import jax.numpy as jnp
from jax import lax
from jax.experimental import pallas as pl
from jax.experimental.pallas import tpu as pltpu

F32 = jnp.float32
BF16 = jnp.bfloat16

DEPTH = 4
HEAD_DIM = 128
GQA_GROUP = 4
GRID_W = 64
ROPE_THETA = 10000.0
ML_HEADS = 8
XA_HEADS = 4
DN_ALPHA = (2 * DEPTH) ** 0.25
LN_EPS = 1e-5
RMS_EPS = 1e-6

V7X_VMEM_BYTES = 64 * 1024 * 1024
VMEM_LIMIT_BYTES = V7X_VMEM_BYTES - 8 * 1024 * 1024
LANES = 128

ML_CHUNK = 128


def _cparams(*sem):
    return pltpu.CompilerParams(dimension_semantics=sem, vmem_limit_bytes=VMEM_LIMIT_BYTES)


def _resident(block_shape, index_map):
    return pl.BlockSpec(block_shape, index_map, pipeline_mode=pl.Buffered(1))


def _deepnorm_ln(res, y, g, b):
    z = DN_ALPHA * res + y
    mu = jnp.mean(z, axis=-1, keepdims=True)
    zc = z - mu
    var = jnp.mean(zc * zc, axis=-1, keepdims=True)
    return zc * lax.rsqrt(var + LN_EPS) * g + b


def _proj_kernel(x_ref, w_ref, *rest, n_extra, epilogue):
    extra = rest[:n_extra]
    o_ref = rest[n_extra]
    xb_ref = rest[n_extra + 1]

    @pl.when(pl.program_id(1) == 0)
    def _():
        xb_ref[...] = x_ref[...].astype(BF16)

    acc = jnp.dot(xb_ref[...], w_ref[...], preferred_element_type=F32)
    epilogue(acc, extra, o_ref)


def _proj(x, w, out_dtype, epilogue, extras=(), extra_specs=(), tm=1024, tn=1024, name="proj"):
    m, k = x.shape
    n = w.shape[1]
    tm = min(tm, m)
    tn = min(tn, n)
    kern = functools.partial(_proj_kernel, n_extra=len(extras), epilogue=epilogue)
    return pl.pallas_call(
        kern,
        grid=(m // tm, n // tn),
        in_specs=[pl.BlockSpec((tm, k), lambda i, j: (i, 0)),
                  pl.BlockSpec((k, tn), lambda i, j: (0, j))] + list(extra_specs),
        out_specs=pl.BlockSpec((tm, tn), lambda i, j: (i, j)),
        out_shape=jax.ShapeDtypeStruct((m, n), out_dtype),
        scratch_shapes=[pltpu.VMEM((tm, k), BF16)],
        compiler_params=_cparams("parallel", "arbitrary"),
        name=name,
    )(x, w, *extras)


def _plain_epilogue(acc, extra, o_ref):
    o_ref[...] = acc.astype(o_ref.dtype)


def _rope_tables(s):
    rows = s // GRID_W
    row_ids = jnp.repeat(jnp.arange(rows), GRID_W).astype(F32)
    col_ids = jnp.tile(jnp.arange(GRID_W), rows).astype(F32)
    axis_dim = HEAD_DIM // 2
    inv_freq = ROPE_THETA ** (-jnp.arange(0, axis_dim, 2, dtype=F32) / axis_dim)
    ang = jnp.concatenate([row_ids[:, None] * inv_freq, col_ids[:, None] * inv_freq], axis=1)
    cos = jnp.concatenate([jnp.cos(ang), jnp.cos(ang)], axis=1)
    sin = jnp.concatenate([-jnp.sin(ang), jnp.sin(ang)], axis=1)
    return cos, sin


def _half_split_heads(a):
    lead = a.shape[:-1]
    a = a.reshape(*lead, -1, 2, 2, HEAD_DIM // 4)
    return jnp.swapaxes(a, -3, -2).reshape(*lead, -1)


def _attn_proj_kernel(x_ref, w_ref, cos_ref, sin_ref, gq_ref, gk_ref, q_ref, kv_ref, xb_ref, acc_ref, *,
                      q_scale, rows):
    j = pl.program_id(1)
    last = pl.num_programs(1) - 1

    @pl.when(j == 0)
    def _():
        xb_ref[...] = x_ref[...].astype(BF16)

    acc_ref[...] = jnp.dot(xb_ref[...], w_ref[...], preferred_element_type=F32)
    tm, tn = acc_ref.shape
    ones = jnp.ones((HEAD_DIM, HEAD_DIM), BF16)

    def norm_rope(o_ref, heads, gains_ref, out_scale):
        g = gains_ref[0:1, :]
        gsw = gains_ref[1:2, :]

        def body(r, carry):
            r0 = pl.multiple_of(r * rows, rows)
            gc = cos_ref[pl.ds(r0, rows), :] * g
            gs = sin_ref[pl.ds(r0, rows), :] * gsw
            for h in range(heads):
                sl = slice(h * HEAD_DIM, (h + 1) * HEAD_DIM)
                a = acc_ref[pl.ds(r0, rows), sl]
                sq = a * a
                sq_hi = sq.astype(BF16)
                sq_lo = (sq - sq_hi.astype(F32)).astype(BF16)
                ssq = (jnp.dot(sq_hi, ones, preferred_element_type=F32)
                       + jnp.dot(sq_lo, ones, preferred_element_type=F32))
                rr = lax.rsqrt(ssq + HEAD_DIM * RMS_EPS) * (out_scale * HEAD_DIM ** 0.5)
                y = (a * gc + pltpu.roll(a, HEAD_DIM // 2, 1) * gs) * rr
                o_ref[pl.ds(r0, rows), sl] = y.astype(o_ref.dtype)
            return carry

        lax.fori_loop(0, tm // rows, body, 0, unroll=2)

    @pl.when(j < last)
    def _():
        norm_rope(q_ref, tn // HEAD_DIM, gq_ref, q_scale)

    @pl.when(j == last)
    def _():
        k_heads = tn // (2 * HEAD_DIM)
        norm_rope(kv_ref, k_heads, gk_ref, 1.0)
        k_cols = k_heads * HEAD_DIM
        for h in range(k_heads):
            src = slice(k_cols + h * HEAD_DIM, k_cols + (h + 1) * HEAD_DIM)
            dst = k_cols + 2 * h * HEAD_DIM
            kv_ref[:, dst:dst + HEAD_DIM] = acc_ref[:, src].astype(kv_ref.dtype)
            kv_ref[:, dst + HEAD_DIM:dst + 2 * HEAD_DIM] = jnp.ones((tm, HEAD_DIM), kv_ref.dtype)


def _attn_proj(x, w, q_gain, k_gain, tables, seq, q_scale, tm=1024, rows=128):
    m, k = x.shape
    tn = w.shape[1] - k
    tm = min(tm, seq)
    rows = min(rows, tm)
    nblk = seq // tm
    nq = k // tn
    cos, sin = tables

    def gains(gain):
        gain = _half_split_heads(gain).reshape(1, HEAD_DIM)
        return jnp.concatenate([gain, jnp.roll(gain, HEAD_DIM // 2, axis=1)], axis=0)

    kern = functools.partial(_attn_proj_kernel, q_scale=q_scale, rows=rows)
    tab_spec = pl.BlockSpec((tm, HEAD_DIM), lambda i, j: (i % nblk, 0))
    g_spec = pl.BlockSpec((2, HEAD_DIM), lambda i, j: (0, 0))
    return pl.pallas_call(
        kern,
        grid=(m // tm, nq + 1),
        in_specs=[pl.BlockSpec((tm, k), lambda i, j: (i, 0)),
                  pl.BlockSpec((k, tn), lambda i, j: (0, j)),
                  tab_spec, tab_spec, g_spec, g_spec],
        out_specs=[pl.BlockSpec((tm, tn), lambda i, j: (i, jnp.minimum(j, nq - 1))),
                   pl.BlockSpec((tm, 3 * tn // 2), lambda i, j: (i, 0))],
        out_shape=[jax.ShapeDtypeStruct((m, k), BF16),
                   jax.ShapeDtypeStruct((m, 3 * tn // 2), BF16)],
        scratch_shapes=[pltpu.VMEM((tm, k), BF16), pltpu.VMEM((tm, tn), F32)],
        compiler_params=_cparams("parallel", "arbitrary"),
        name="attn_qkv_proj",
    )(x, w, cos, sin, gains(q_gain), gains(k_gain))


def _flash_kernel(q_ref, k_ref, v_ref, o_ref, q4_ref, *, tq, tk):
    seq = k_ref.shape[0]
    for h in range(GQA_GROUP):
        q4_ref[h * tq:(h + 1) * tq, :] = q_ref[:, h * HEAD_DIM:(h + 1) * HEAD_DIM]
    q4 = q4_ref[...]
    rows = GQA_GROUP * tq

    m = jnp.full((rows, 1), -jnp.inf, F32)
    acc = jnp.zeros((rows, 2 * HEAD_DIM), F32)
    for c in range(seq // tk):
        kc = k_ref[c * tk:(c + 1) * tk, :]
        vc = v_ref[c * tk:(c + 1) * tk, :]
        s = lax.dot_general(q4, kc, (((1,), (1,)), ((), ())), preferred_element_type=F32)
        m_new = jnp.maximum(m, jnp.max(s, axis=-1, keepdims=True))
        p = jnp.exp2(s - m_new).astype(BF16)
        acc = jnp.exp2(m - m_new) * acc + jnp.dot(p, vc, preferred_element_type=F32)
        m = m_new
    out = acc[:, :HEAD_DIM] / acc[:, HEAD_DIM:]
    for h in range(GQA_GROUP):
        o_ref[:, h * HEAD_DIM:(h + 1) * HEAD_DIM] = out[h * tq:(h + 1) * tq, :].astype(o_ref.dtype)


def _flash_attention(q, kv, batch, seq, tq=512, tk=256):
    t, dq = q.shape
    nkv = dq // HEAD_DIM // GQA_GROUP
    tq = min(tq, seq)
    tk = min(tk, seq)
    nq = seq // tq
    gw = GQA_GROUP * HEAD_DIM
    v0 = nkv // 2
    kern = functools.partial(_flash_kernel, tq=tq, tk=tk)
    return pl.pallas_call(
        kern,
        grid=(batch, nkv, nq),
        in_specs=[pl.BlockSpec((tq, gw), lambda b, g, i: (b * nq + i, g)),
                  pl.BlockSpec((seq, HEAD_DIM), lambda b, g, i: (b, g)),
                  pl.BlockSpec((seq, 2 * HEAD_DIM), lambda b, g, i: (b, v0 + g))],
        out_specs=pl.BlockSpec((tq, gw), lambda b, g, i: (b * nq + i, g)),
        out_shape=jax.ShapeDtypeStruct((t, dq), BF16),
        scratch_shapes=[pltpu.VMEM((GQA_GROUP * tq, HEAD_DIM), BF16)],
        compiler_params=_cparams("parallel", "parallel", "arbitrary"),
        name="flash_attention",
    )(q, kv, kv)


def _out_ln_kernel(a_ref, w_ref, res_ref, g_ref, b_ref, o_ref):
    y = jnp.dot(a_ref[...], w_ref[...], preferred_element_type=F32)
    o_ref[...] = _deepnorm_ln(res_ref[...], y, g_ref[...], b_ref[...])


def _out_ln(a, w, res, g, b, tm=512):
    m, k = a.shape
    d = w.shape[1]
    tm = min(tm, m)
    return pl.pallas_call(
        _out_ln_kernel,
        grid=(m // tm,),
        in_specs=[pl.BlockSpec((tm, k), lambda i: (i, 0)),
                  _resident((k, d), lambda i: (0, 0)),
                  pl.BlockSpec((tm, d), lambda i: (i, 0)),
                  _resident((1, d), lambda i: (0, 0)),
                  _resident((1, d), lambda i: (0, 0))],
        out_specs=pl.BlockSpec((tm, d), lambda i: (i, 0)),
        out_shape=jax.ShapeDtypeStruct((m, d), F32),
        compiler_params=_cparams("parallel"),
        name="attn_out_ln",
    )(a, w, res, g.reshape(1, d), b.reshape(1, d))


def _ml_in_kernel(x_ref, w_ref, wg_ref, qkv_ref, gates_ref, xb_ref, *, k_scale):
    j = pl.program_id(1)

    @pl.when(j == 0)
    def _():
        xb_ref[...] = x_ref[...].astype(BF16)
        gates_ref[...] = jnp.dot(xb_ref[...], wg_ref[...], preferred_element_type=F32)

    acc = jnp.dot(xb_ref[...], w_ref[...], preferred_element_type=F32)
    scale = jnp.where(j == 1, k_scale, 1.0).astype(F32)
    qkv_ref[...] = (acc * scale).astype(qkv_ref.dtype)


def _ml_in_proj(x, w_qkv, w_g, k_scale, tn, tm=1024):
    m, k = x.shape
    n = w_qkv.shape[1]
    tm = min(tm, m)
    kern = functools.partial(_ml_in_kernel, k_scale=k_scale)
    return pl.pallas_call(
        kern,
        grid=(m // tm, n // tn),
        in_specs=[pl.BlockSpec((tm, k), lambda i, j: (i, 0)),
                  pl.BlockSpec((k, tn), lambda i, j: (0, j)),
                  _resident((k, LANES), lambda i, j: (0, 0))],
        out_specs=[pl.BlockSpec((tm, tn), lambda i, j: (i, j)),
                   pl.BlockSpec((tm, LANES), lambda i, j: (i, 0))],
        out_shape=[jax.ShapeDtypeStruct((m, n), BF16), jax.ShapeDtypeStruct((m, LANES), F32)],
        scratch_shapes=[pltpu.VMEM((tm, k), BF16)],
        compiler_params=_cparams("parallel", "arbitrary"),
        name="mlstm_in_proj",
    )(x, w_qkv, w_g)


def _log_sigmoid(x):
    return jnp.minimum(x, 0.0) - jnp.log(1.0 + jnp.exp(-jnp.abs(x)))


def _lane_scan(x, op, fill, backward):
    length = x.shape[1]
    lane = lax.broadcasted_iota(jnp.int32, x.shape, 1)
    k = 1
    while k < length:
        if backward:
            x = op(x, jnp.where(lane < length - k, pltpu.roll(x, length - k, 1), fill))
        else:
            x = op(x, jnp.where(lane >= k, pltpu.roll(x, k, 1), fill))
        k *= 2
    return x


def _gate_prework(gates, backward):
    nh = ML_HEADS
    length = gates.shape[1]
    f_rows = _log_sigmoid(gates[nh:])
    b_rows = _lane_scan(f_rows, jnp.add, 0.0, backward)
    a_rows = gates[:nh] - b_rows
    cm_rows = _lane_scan(a_rows, jnp.maximum, -jnp.inf, backward)
    b_tot = jnp.broadcast_to(jnp.sum(f_rows, axis=1, keepdims=True), (nh, length))
    a_max = jnp.broadcast_to(jnp.max(a_rows, axis=1, keepdims=True), (nh, length))
    rows = jnp.concatenate([a_rows, b_tot, a_max, jnp.zeros((nh, length), F32)], axis=0)
    cols = jnp.concatenate([cm_rows, b_rows, jnp.zeros((length - 2 * nh, length), F32)], axis=0).T
    return rows, cols


def _mlstm_direction(backward, q_ref, kt_ref, v_ref, gn_ref, bias, o_ref, c_ref, m_ref, rows_ref, cols_ref,
                     dqk, dv):
    chunk = q_ref.shape[0]
    nh = ML_HEADS
    rows = rows_ref[...]
    cols = cols_ref[...]
    rows_next, cols_next = _gate_prework(gn_ref[0, 0] + bias, backward)
    rows_ref[...] = rows_next
    cols_ref[...] = cols_next

    row = lax.broadcasted_iota(jnp.int32, (chunk, chunk), 0)
    col = lax.broadcasted_iota(jnp.int32, (chunk, chunk), 1)
    mask = (col >= row) if backward else (col <= row)

    a_rows = rows[:nh]
    m_prev_all = m_ref[:, 0:1]
    m_x_all = jnp.maximum(m_prev_all, rows[2 * nh:3 * nh, 0:1])
    g_s_all = jnp.exp(m_prev_all - m_x_all)
    ws_rows = jnp.exp(a_rows - m_x_all)
    m_ref[...] = jnp.broadcast_to(rows[nh:2 * nh, 0:1] + m_x_all, m_ref.shape)
    ones = jnp.ones((chunk, LANES), BF16)

    for h in range(nh):
        m_prev = m_prev_all[h:h + 1, :]
        cm_col = jnp.broadcast_to(cols[:, h:h + 1], (chunk, LANES))
        b_col = jnp.broadcast_to(cols[:, nh + h:nh + h + 1], (chunk, LANES))
        m_col = jnp.maximum(m_prev, cm_col)
        floor = jnp.exp(-(b_col + m_col))
        w_mat = jnp.exp(jnp.where(mask, a_rows[h:h + 1, :], -jnp.inf) - m_col)
        g_col = jnp.exp(m_prev - m_col)

        qh = q_ref[:, h * dqk:(h + 1) * dqk]
        kth = kt_ref[h * dqk:(h + 1) * dqk, :]
        v_ext = jnp.concatenate([v_ref[:, h * dv:(h + 1) * dv], ones], axis=1)
        ct = c_ref[h]
        s = jnp.dot(qh, kth, preferred_element_type=F32) * w_mat
        inter = jnp.dot(qh, ct.astype(BF16), preferred_element_type=F32)
        intra = jnp.dot(s.astype(BF16), v_ext, preferred_element_type=F32)
        den = jnp.maximum(jnp.abs(g_col * inter[:, dv:] + intra[:, dv:]), floor)
        for part in range(dv // LANES):
            sl = slice(part * LANES, (part + 1) * LANES)
            o_ref[:, h * dv + part * LANES:h * dv + (part + 1) * LANES] = (
                (g_col * inter[:, sl] + intra[:, sl]) / den)

        kts = (kth.astype(F32) * ws_rows[h:h + 1, :]).astype(BF16)
        c_ref[h] = g_s_all[h:h + 1, :] * ct + jnp.dot(kts, v_ext, preferred_element_type=F32)


def _mlstm_kernel(qf_ref, ktf_ref, vf_ref, gf_ref, gnf_ref, qb_ref, ktb_ref, vb_ref, gb_ref, gnb_ref, bias_ref,
                  of_ref, ob_ref, c_ref, m_ref, rows_ref, cols_ref, *, dqk, dv):
    @pl.when(pl.program_id(1) == 0)
    def _():
        c_ref[...] = jnp.zeros_like(c_ref)
        m_ref[...] = jnp.zeros_like(m_ref)
        for d, g_ref in enumerate((gf_ref, gb_ref)):
            rows0, cols0 = _gate_prework(g_ref[0, 0] + bias_ref[d], d == 1)
            rows_ref[d] = rows0
            cols_ref[d] = cols0

    dirs = ((qf_ref, ktf_ref, vf_ref, gnf_ref, of_ref), (qb_ref, ktb_ref, vb_ref, gnb_ref, ob_ref))
    for d, (q_ref, kt_ref, v_ref, gn_ref, o_ref) in enumerate(dirs):
        _mlstm_direction(d == 1, q_ref, kt_ref, v_ref, gn_ref, bias_ref[d], o_ref, c_ref.at[d], m_ref.at[d],
                         rows_ref.at[d], cols_ref.at[d], dqk, dv)


def _mlstm(qkv, kt, gates_rows, bias_rows, batch, seq):
    t = qkv.shape[0]
    dv = qkv.shape[1] // (2 * ML_HEADS)
    dqk = dv // 2
    chunk = ML_CHUNK
    assert chunk == LANES and seq % chunk == 0
    nc = seq // chunk
    wqk = ML_HEADS * dqk
    wv = ML_HEADS * dv

    def specs(d):
        def cb(c):
            return nc - 1 - c if d else c

        def cn(c):
            return cb(jnp.minimum(c + 1, nc - 1))

        return [pl.BlockSpec((chunk, wqk), lambda b, c: (b * nc + cb(c), 0)),
                pl.BlockSpec((wqk, chunk), lambda b, c: (b, cb(c))),
                pl.BlockSpec((chunk, wv), lambda b, c: (b * nc + cb(c), 1)),
                pl.BlockSpec((1, 1, 2 * ML_HEADS, chunk), lambda b, c: (b, d, 0, cb(c))),
                pl.BlockSpec((1, 1, 2 * ML_HEADS, chunk), lambda b, c: (b, d, 0, cn(c)))]

    kern = functools.partial(_mlstm_kernel, dqk=dqk, dv=dv)
    return pl.pallas_call(
        kern,
        grid=(batch, nc),
        in_specs=specs(0) + specs(1) + [pl.BlockSpec((2, 2 * ML_HEADS, 1), lambda b, c: (0, 0, 0))],
        out_specs=[pl.BlockSpec((chunk, wv), lambda b, c: (b * nc + c, 0)),
                   pl.BlockSpec((chunk, wv), lambda b, c: (b * nc + nc - 1 - c, 0))],
        out_shape=[jax.ShapeDtypeStruct((t, wv), F32), jax.ShapeDtypeStruct((t, wv), F32)],
        scratch_shapes=[pltpu.VMEM((2, ML_HEADS, dqk, dv + LANES), F32),
                        pltpu.VMEM((2, ML_HEADS, LANES), F32),
                        pltpu.VMEM((2, 4 * ML_HEADS, chunk), F32),
                        pltpu.VMEM((2, chunk, chunk), F32)],
        compiler_params=_cparams("parallel", "arbitrary"),
        name="mlstm_chunks",
    )(qkv, kt, qkv, gates_rows, gates_rows, qkv, kt, qkv, gates_rows, gates_rows, bias_rows)


def _ml_out_kernel(hf_ref, hb_ref, x_ref, wo_ref, gain_ref, w_ref, g_ref, b_ref, o_ref, a_ref, *, dv):
    x = x_ref[...]
    og = jnp.dot(x.astype(BF16), wo_ref[...], preferred_element_type=F32)
    for h in range(ML_HEADS):
        sl = slice(h * dv, (h + 1) * dv)
        hs = hf_ref[:, sl] + hb_ref[:, sl]
        ms = jnp.mean(hs * hs, axis=-1, keepdims=True)
        hn = hs * lax.rsqrt(ms + RMS_EPS) * gain_ref[:, sl]
        a_ref[:, sl] = (hn * jax.nn.sigmoid(og[:, sl])).astype(BF16)
    y = jnp.dot(a_ref[...], w_ref[...], preferred_element_type=F32)
    o_ref[...] = _deepnorm_ln(x, y, g_ref[...], b_ref[...])


def _ml_out(h_fwd, h_bwd, x, w_ogate, gain, w, g, b, tm=256):
    m, d = h_fwd.shape
    tm = min(tm, m)
    kern = functools.partial(_ml_out_kernel, dv=d // ML_HEADS)
    row = lambda i: (i, 0)
    const = lambda i: (0, 0)
    return pl.pallas_call(
        kern,
        grid=(m // tm,),
        in_specs=[pl.BlockSpec((tm, d), row),
                  pl.BlockSpec((tm, d), row),
                  pl.BlockSpec((tm, d), row),
                  _resident((d, d), const),
                  _resident((1, d), const),
                  _resident((d, d), const),
                  _resident((1, d), const),
                  _resident((1, d), const)],
        out_specs=pl.BlockSpec((tm, d), row),
        out_shape=jax.ShapeDtypeStruct((m, d), F32),
        scratch_shapes=[pltpu.VMEM((tm, d), BF16)],
        compiler_params=_cparams("parallel"),
        name="mlstm_out_ln",
    )(h_fwd, h_bwd, x, w_ogate, gain.reshape(1, d), w, g.reshape(1, d), b.reshape(1, d))


def _xattn_fold_kernel(kv_ref, wq_ref, wo_ref, wqk_ref, vo_ref):
    mem_len = kv_ref.shape[0]
    d = wq_ref.shape[0]
    hd = d // XA_HEADS
    for h in range(XA_HEADS):
        hs = slice(h * hd, (h + 1) * hd)
        ms = slice(h * mem_len, (h + 1) * mem_len)
        wqk_ref[0, :, ms] = lax.dot_general(wq_ref[:, hs], kv_ref[:, hs], (((1,), (1,)), ((), ())),
                                            preferred_element_type=F32).astype(wqk_ref.dtype)
        vo_ref[0, ms, :] = jnp.dot(kv_ref[:, d + h * hd:d + (h + 1) * hd], wo_ref[hs, :],
                                   preferred_element_type=F32).astype(vo_ref.dtype)


def _xattn_fold(kv, wq, wo, batch):
    d = wq.shape[0]
    mem_len = kv.shape[0] // batch
    hm = XA_HEADS * mem_len
    return pl.pallas_call(
        _xattn_fold_kernel,
        grid=(batch,),
        in_specs=[pl.BlockSpec((mem_len, 2 * d), lambda bi: (bi, 0)),
                  _resident((d, d), lambda bi: (0, 0)),
                  _resident((d, d), lambda bi: (0, 0))],
        out_specs=[pl.BlockSpec((1, d, hm), lambda bi: (bi, 0, 0)),
                   pl.BlockSpec((1, hm, d), lambda bi: (bi, 0, 0))],
        out_shape=[jax.ShapeDtypeStruct((batch, d, hm), BF16),
                   jax.ShapeDtypeStruct((batch, hm, d), BF16)],
        compiler_params=_cparams("parallel"),
        name="xattn_fold",
    )(kv, wq, wo)


def _xattn_kernel(x_ref, wqk_ref, vo_ref, g_ref, b_ref, o_ref, p_ref, *, mem_len):
    x = x_ref[...]
    scale = (x.shape[1] // XA_HEADS) ** -0.5
    s = jnp.dot(x.astype(BF16), wqk_ref[0], preferred_element_type=F32) * scale
    for h in range(XA_HEADS):
        ms = slice(h * mem_len, (h + 1) * mem_len)
        sh = s[:, ms]
        e = jnp.exp(sh - jnp.max(sh, axis=-1, keepdims=True))
        p_ref[:, ms] = (e / jnp.sum(e, axis=-1, keepdims=True)).astype(BF16)
    y = jnp.dot(p_ref[...], vo_ref[0], preferred_element_type=F32)
    o_ref[...] = _deepnorm_ln(x, y, g_ref[...], b_ref[...])


def _xattn(x, wqk, vo, g, b, batch, seq, tq=512):
    t, d = x.shape
    hm = wqk.shape[2]
    tq = min(tq, seq)
    nq = seq // tq
    const = lambda bi, i: (0, 0)
    kern = functools.partial(_xattn_kernel, mem_len=hm // XA_HEADS)
    return pl.pallas_call(
        kern,
        grid=(batch, nq),
        in_specs=[pl.BlockSpec((tq, d), lambda bi, i: (bi * nq + i, 0)),
                  pl.BlockSpec((1, d, hm), lambda bi, i: (bi, 0, 0)),
                  pl.BlockSpec((1, hm, d), lambda bi, i: (bi, 0, 0)),
                  _resident((1, d), const),
                  _resident((1, d), const)],
        out_specs=pl.BlockSpec((tq, d), lambda bi, i: (bi * nq + i, 0)),
        out_shape=jax.ShapeDtypeStruct((t, d), F32),
        scratch_shapes=[pltpu.VMEM((tq, hm), BF16)],
        compiler_params=_cparams("parallel", "arbitrary"),
        name="xattn_sublayer",
    )(x, wqk, vo, g.reshape(1, d), b.reshape(1, d))


def _mlp_kernel(x_ref, w1_ref, w2_ref, g_ref, b_ref, o_ref, xb_ref):
    f = pl.program_id(1)

    @pl.when(f == 0)
    def _():
        xb_ref[...] = x_ref[...].astype(BF16)
        o_ref[...] = jnp.zeros_like(o_ref)

    h = jnp.dot(xb_ref[...], w1_ref[...], preferred_element_type=F32)
    h = jnp.maximum(h, 0.0)
    o_ref[...] += jnp.dot((h * h).astype(BF16), w2_ref[...], preferred_element_type=F32)

    @pl.when(f == pl.num_programs(1) - 1)
    def _():
        o_ref[...] = _deepnorm_ln(x_ref[...], o_ref[...], g_ref[...], b_ref[...])


def _mlp(x, w1, w2, g, b, tm=1024, tf=512):
    m, d = x.shape
    dff = w1.shape[1]
    tm = min(tm, m)
    tf = min(tf, dff)
    const = lambda i, f: (0, 0)
    return pl.pallas_call(
        _mlp_kernel,
        grid=(m // tm, dff // tf),
        in_specs=[pl.BlockSpec((tm, d), lambda i, f: (i, 0)),
                  pl.BlockSpec((d, tf), lambda i, f: (0, f)),
                  pl.BlockSpec((tf, d), lambda i, f: (f, 0)),
                  _resident((1, d), const),
                  _resident((1, d), const)],
        out_specs=pl.BlockSpec((tm, d), lambda i, f: (i, 0)),
        out_shape=jax.ShapeDtypeStruct((m, d), F32),
        scratch_shapes=[pltpu.VMEM((tm, d), BF16)],
        compiler_params=_cparams("parallel", "arbitrary"),
        name="mlp_sublayer",
    )(x, w1, w2, g.reshape(1, d), b.reshape(1, d))


def _gqa_sublayer(x, batch, seq, w_in, q_gain, k_gain, w_out, ln_g, ln_b):
    d = x.shape[1]
    tables = _rope_tables(seq)
    nkv = d // HEAD_DIM // GQA_GROUP
    kd = nkv * HEAD_DIM
    w = jnp.concatenate([_half_split_heads(w_in[:, :d + kd]), w_in[:, d + kd:]], axis=1).astype(BF16)
    q_scale = HEAD_DIM ** -0.5 * math.log2(math.e)
    q, kv = _attn_proj(x, w, q_gain, k_gain, tables, seq, q_scale)
    o = _flash_attention(q, kv, batch, seq)
    return _out_ln(o, w_out.astype(BF16), x, ln_g, ln_b)


def _mlstm_sublayer(x, batch, seq, w_in, b_gate, head_gain, w_out, ln_g, ln_b):
    t, d = x.shape
    dv = d // ML_HEADS
    dqk = dv // 2
    nqk = ML_HEADS * dqk
    w_qkv = w_in[:, :2 * nqk + d].astype(BF16)
    w_o = w_in[:, 2 * nqk + d:2 * nqk + 2 * d].astype(BF16)
    ng = 4 * ML_HEADS
    w_g = jnp.pad(w_in[:, 2 * nqk + 2 * d:], ((0, 0), (0, LANES - ng))).astype(BF16)
    qkv, gates = _ml_in_proj(x, w_qkv, w_g, dqk ** -0.5, nqk)
    gates = gates[:, :ng].reshape(batch, seq, 2, 2 * ML_HEADS).transpose(0, 2, 3, 1)
    kt = qkv[:, nqk:2 * nqk].reshape(batch, seq, nqk).transpose(0, 2, 1).reshape(batch * nqk, seq)
    bias = b_gate.astype(F32).reshape(2, 2 * ML_HEADS, 1)
    h_fwd, h_bwd = _mlstm(qkv, kt, gates, bias, batch, seq)
    return _ml_out(h_fwd, h_bwd, x, w_o, head_gain.reshape(-1), w_out.astype(BF16), ln_g, ln_b)


def _trunk(x3, mem3, p):
    batch, seq, d = x3.shape
    x = x3.reshape(batch * seq, d)
    mem = mem3.reshape(-1, d)
    for i in range(DEPTH):
        j = i // 2
        if i % 2 == 0:
            x = _gqa_sublayer(x, batch, seq, p['attn_w_in'][j], p['attn_q_gain'][j], p['attn_k_gain'][j],
                              p['attn_w_out'][j], p['ln_g'][i, 0], p['ln_b'][i, 0])
        else:
            x = _mlstm_sublayer(x, batch, seq, p['ml_w_in'][j], p['ml_b_gate'][j], p['ml_head_gain'][j],
                                p['ml_w_out'][j], p['ln_g'][i, 0], p['ln_b'][i, 0])
        kv = _proj(mem, p['xa_w_kv'][i].astype(BF16), BF16, _plain_epilogue, name="xattn_kv_proj")
        wqk, vo = _xattn_fold(kv, p['xa_w_q'][i].astype(BF16), p['xa_w_out'][i].astype(BF16), batch)
        x = _xattn(x, wqk, vo, p['ln_g'][i, 1], p['ln_b'][i, 1], batch, seq)
        x = _mlp(x, p['mlp_w1'][i].astype(BF16), p['mlp_w2'][i].astype(BF16),
                 p['ln_g'][i, 2], p['ln_b'][i, 2])
    return x.reshape(batch, seq, d)


def kernel(x_prompt, x_sample, mem_prompt, mem_sample, attn_w_in, attn_q_gain, attn_k_gain, attn_w_out,
           ml_w_in, ml_b_gate, ml_head_gain, ml_w_out, xa_w_q, xa_w_kv, xa_w_out, mlp_w1, mlp_w2,
           ln_g, ln_b):
    params = {
        'attn_w_in': attn_w_in, 'attn_q_gain': attn_q_gain, 'attn_k_gain': attn_k_gain,
        'attn_w_out': attn_w_out, 'ml_w_in': ml_w_in, 'ml_b_gate': ml_b_gate,
        'ml_head_gain': ml_head_gain, 'ml_w_out': ml_w_out, 'xa_w_q': xa_w_q, 'xa_w_kv': xa_w_kv,
        'xa_w_out': xa_w_out, 'mlp_w1': mlp_w1, 'mlp_w2': mlp_w2, 'ln_g': ln_g, 'ln_b': ln_b,
    }
    y_prompt = _trunk(x_prompt, mem_prompt, params)
    y_sample = _trunk(x_sample, mem_sample, params)
    return (y_prompt, y_sample)
```

```python
import functools
import math

import jax
import jax.numpy as jnp
from jax import lax
from jax.experimental import pallas as pl
from jax.experimental.pallas import tpu as pltpu

F32 = jnp.float32
BF16 = jnp.bfloat16

DEPTH = 4
HEAD_DIM = 128
GQA_GROUP = 4
GRID_W = 64
ROPE_THETA = 10000.0
ML_HEADS = 8
XA_HEADS = 4
DN_ALPHA = (2 * DEPTH) ** 0.25
LN_EPS = 1e-5
RMS_EPS = 1e-6

V7X_VMEM_BYTES = 64 * 1024 * 1024
VMEM_LIMIT_BYTES = V7X_VMEM_BYTES - 8 * 1024 * 1024
LANES = 128

ML_CHUNK = 128


def _cparams(*sem):
    return pltpu.CompilerParams(dimension_semantics=sem, vmem_limit_bytes=VMEM_LIMIT_BYTES)


def _resident(block_shape, index_map):
    return pl.BlockSpec(block_shape, index_map, pipeline_mode=pl.Buffered(1))


def _deepnorm_ln(res, y, g, b):
    z = DN_ALPHA * res + y
    mu = jnp.mean(z, axis=-1, keepdims=True)
    zc = z - mu
    var = jnp.mean(zc * zc, axis=-1, keepdims=True)
    return zc * lax.rsqrt(var + LN_EPS) * g + b


def _proj_kernel(x_ref, w_ref, *rest, n_extra, epilogue):
    extra = rest[:n_extra]
    o_ref = rest[n_extra]
    xb_ref = rest[n_extra + 1]

    @pl.when(pl.program_id(1) == 0)
    def _():
        xb_ref[...] = x_ref[...].astype(BF16)

    acc = jnp.dot(xb_ref[...], w_ref[...], preferred_element_type=F32)
    epilogue(acc, extra, o_ref)


def _proj(x, w, out_dtype, epilogue, extras=(), extra_specs=(), tm=1024, tn=1024, name="proj"):
    m, k = x.shape
    n = w.shape[1]
    tm = min(tm, m)
    tn = min(tn, n)
    kern = functools.partial(_proj_kernel, n_extra=len(extras), epilogue=epilogue)
    return pl.pallas_call(
        kern,
        grid=(m // tm, n // tn),
        in_specs=[pl.BlockSpec((tm, k), lambda i, j: (i, 0)),
                  pl.BlockSpec((k, tn), lambda i, j: (0, j))] + list(extra_specs),
        out_specs=pl.BlockSpec((tm, tn), lambda i, j: (i, j)),
        out_shape=jax.ShapeDtypeStruct((m, n), out_dtype),
        scratch_shapes=[pltpu.VMEM((tm, k), BF16)],
        compiler_params=_cparams("parallel", "arbitrary"),
        name=name,
    )(x, w, *extras)


def _plain_epilogue(acc, extra, o_ref):
    o_ref[...] = acc.astype(o_ref.dtype)


def _rope_tables(s):
    rows = s // GRID_W
    row_ids = jnp.repeat(jnp.arange(rows), GRID_W).astype(F32)
    col_ids = jnp.tile(jnp.arange(GRID_W), rows).astype(F32)
    axis_dim = HEAD_DIM // 2
    inv_freq = ROPE_THETA ** (-jnp.arange(0, axis_dim, 2, dtype=F32) / axis_dim)
    ang = jnp.concatenate([row_ids[:, None] * inv_freq, col_ids[:, None] * inv_freq], axis=1)
    cos = jnp.concatenate([jnp.cos(ang), jnp.cos(ang)], axis=1)
    sin = jnp.concatenate([-jnp.sin(ang), jnp.sin(ang)], axis=1)
    return cos, sin


def _half_split_heads(a):
    lead = a.shape[:-1]
    a = a.reshape(*lead, -1, 2, 2, HEAD_DIM // 4)
    return jnp.swapaxes(a, -3, -2).reshape(*lead, -1)


def _attn_proj_kernel(x_ref, w_ref, cos_ref, sin_ref, gq_ref, gk_ref, q_ref, kv_ref, xb_ref, acc_ref, *,
                      q_scale, rows):
    j = pl.program_id(1)
    last = pl.num_programs(1) - 1

    @pl.when(j == 0)
    def _():
        xb_ref[...] = x_ref[...].astype(BF16)

    acc_ref[...] = jnp.dot(xb_ref[...], w_ref[...], preferred_element_type=F32)
    tm, tn = acc_ref.shape
    ones = jnp.ones((HEAD_DIM, HEAD_DIM), BF16)

    def norm_rope(o_ref, heads, gains_ref, out_scale):
        g = gains_ref[0:1, :]
        gsw = gains_ref[1:2, :]

        def body(r, carry):
            r0 = pl.multiple_of(r * rows, rows)
            gc = cos_ref[pl.ds(r0, rows), :] * g
            gs = sin_ref[pl.ds(r0, rows), :] * gsw
            for h in range(heads):
                sl = slice(h * HEAD_DIM, (h + 1) * HEAD_DIM)
                a = acc_ref[pl.ds(r0, rows), sl]
                ssq = jnp.dot((a * a).astype(BF16), ones, preferred_element_type=F32)
                rr = lax.rsqrt(ssq + HEAD_DIM * RMS_EPS) * (out_scale * HEAD_DIM ** 0.5)
                y = (a * gc + pltpu.roll(a, HEAD_DIM // 2, 1) * gs) * rr
                o_ref[pl.ds(r0, rows), sl] = y.astype(o_ref.dtype)
            return carry

        lax.fori_loop(0, tm // rows, body, 0, unroll=2)

    @pl.when(j < last)
    def _():
        norm_rope(q_ref, tn // HEAD_DIM, gq_ref, q_scale)

    @pl.when(j == last)
    def _():
        k_heads = tn // (2 * HEAD_DIM)
        norm_rope(kv_ref, k_heads, gk_ref, 1.0)
        k_cols = k_heads * HEAD_DIM
        for h in range(k_heads):
            src = slice(k_cols + h * HEAD_DIM, k_cols + (h + 1) * HEAD_DIM)
            dst = k_cols + 2 * h * HEAD_DIM
            kv_ref[:, dst:dst + HEAD_DIM] = acc_ref[:, src].astype(kv_ref.dtype)
            kv_ref[:, dst + HEAD_DIM:dst + 2 * HEAD_DIM] = jnp.ones((tm, HEAD_DIM), kv_ref.dtype)


def _attn_proj(x, w, q_gain, k_gain, tables, seq, q_scale, tm=1024, rows=128):
    m, k = x.shape
    tn = w.shape[1] - k
    tm = min(tm, seq)
    rows = min(rows, tm)
    nblk = seq // tm
    nq = k // tn
    cos, sin = tables

    def gains(gain):
        gain = _half_split_heads(gain).reshape(1, HEAD_DIM)
        return jnp.concatenate([gain, jnp.roll(gain, HEAD_DIM // 2, axis=1)], axis=0)

    kern = functools.partial(_attn_proj_kernel, q_scale=q_scale, rows=rows)
    tab_spec = pl.BlockSpec((tm, HEAD_DIM), lambda i, j: (i % nblk, 0))
    g_spec = pl.BlockSpec((2, HEAD_DIM), lambda i, j: (0, 0))
    return pl.pallas_call(
        kern,
        grid=(m // tm, nq + 1),
        in_specs=[pl.BlockSpec((tm, k), lambda i, j: (i, 0)),
                  pl.BlockSpec((k, tn), lambda i, j: (0, j)),
                  tab_spec, tab_spec, g_spec, g_spec],
        out_specs=[pl.BlockSpec((tm, tn), lambda i, j: (i, jnp.minimum(j, nq - 1))),
                   pl.BlockSpec((tm, 3 * tn // 2), lambda i, j: (i, 0))],
        out_shape=[jax.ShapeDtypeStruct((m, k), BF16),
                   jax.ShapeDtypeStruct((m, 3 * tn // 2), BF16)],
        scratch_shapes=[pltpu.VMEM((tm, k), BF16), pltpu.VMEM((tm, tn), F32)],
        compiler_params=_cparams("parallel", "arbitrary"),
        name="attn_qkv_proj",
    )(x, w, cos, sin, gains(q_gain), gains(k_gain))


def _flash_kernel(q_ref, k_ref, v_ref, o_ref, q4_ref, *, tq, tk):
    seq = k_ref.shape[0]
    for h in range(GQA_GROUP):
        q4_ref[h * tq:(h + 1) * tq, :] = q_ref[:, h * HEAD_DIM:(h + 1) * HEAD_DIM]
    q4 = q4_ref[...]
    rows = GQA_GROUP * tq

    m = jnp.full((rows, 1), -jnp.inf, F32)
    acc = jnp.zeros((rows, 2 * HEAD_DIM), F32)
    for c in range(seq // tk):
        kc = k_ref[c * tk:(c + 1) * tk, :]
        vc = v_ref[c * tk:(c + 1) * tk, :]
        s = lax.dot_general(q4, kc, (((1,), (1,)), ((), ())), preferred_element_type=F32)
        m_new = jnp.maximum(m, jnp.max(s, axis=-1, keepdims=True))
        p = jnp.exp2(s - m_new).astype(BF16)
        acc = jnp.exp2(m - m_new) * acc + jnp.dot(p, vc, preferred_element_type=F32)
        m = m_new
    out = acc[:, :HEAD_DIM] / acc[:, HEAD_DIM:]
    for h in range(GQA_GROUP):
        o_ref[:, h * HEAD_DIM:(h + 1) * HEAD_DIM] = out[h * tq:(h + 1) * tq, :].astype(o_ref.dtype)


def _flash_attention(q, kv, batch, seq, tq=512, tk=256):
    t, dq = q.shape
    nkv = dq // HEAD_DIM // GQA_GROUP
    tq = min(tq, seq)
    tk = min(tk, seq)
    nq = seq // tq
    gw = GQA_GROUP * HEAD_DIM
    v0 = nkv // 2
    kern = functools.partial(_flash_kernel, tq=tq, tk=tk)
    return pl.pallas_call(
        kern,
        grid=(batch, nkv, nq),
        in_specs=[pl.BlockSpec((tq, gw), lambda b, g, i: (b * nq + i, g)),
                  pl.BlockSpec((seq, HEAD_DIM), lambda b, g, i: (b, g)),
                  pl.BlockSpec((seq, 2 * HEAD_DIM), lambda b, g, i: (b, v0 + g))],
        out_specs=pl.BlockSpec((tq, gw), lambda b, g, i: (b * nq + i, g)),
        out_shape=jax.ShapeDtypeStruct((t, dq), BF16),
        scratch_shapes=[pltpu.VMEM((GQA_GROUP * tq, HEAD_DIM), BF16)],
        compiler_params=_cparams("parallel", "parallel", "arbitrary"),
        name="flash_attention",
    )(q, kv, kv)


def _out_ln_kernel(a_ref, w_ref, res_ref, g_ref, b_ref, o_ref):
    y = jnp.dot(a_ref[...], w_ref[...], preferred_element_type=F32)
    o_ref[...] = _deepnorm_ln(res_ref[...], y, g_ref[...], b_ref[...])


def _out_ln(a, w, res, g, b, tm=512):
    m, k = a.shape
    d = w.shape[1]
    tm = min(tm, m)
    return pl.pallas_call(
        _out_ln_kernel,
        grid=(m // tm,),
        in_specs=[pl.BlockSpec((tm, k), lambda i: (i, 0)),
                  _resident((k, d), lambda i: (0, 0)),
                  pl.BlockSpec((tm, d), lambda i: (i, 0)),
                  _resident((1, d), lambda i: (0, 0)),
                  _resident((1, d), lambda i: (0, 0))],
        out_specs=pl.BlockSpec((tm, d), lambda i: (i, 0)),
        out_shape=jax.ShapeDtypeStruct((m, d), F32),
        compiler_params=_cparams("parallel"),
        name="attn_out_ln",
    )(a, w, res, g.reshape(1, d), b.reshape(1, d))


def _ml_in_kernel(x_ref, w_ref, wg_ref, qkv_ref, gates_ref, xb_ref, *, k_scale):
    j = pl.program_id(1)

    @pl.when(j == 0)
    def _():
        xb_ref[...] = x_ref[...].astype(BF16)
        gates_ref[...] = jnp.dot(xb_ref[...], wg_ref[...], preferred_element_type=F32)

    acc = jnp.dot(xb_ref[...], w_ref[...], preferred_element_type=F32)
    scale = jnp.where(j == 1, k_scale, 1.0).astype(F32)
    qkv_ref[...] = (acc * scale).astype(qkv_ref.dtype)


def _ml_in_proj(x, w_qkv, w_g, k_scale, tn, tm=1024):
    m, k = x.shape
    n = w_qkv.shape[1]
    tm = min(tm, m)
    kern = functools.partial(_ml_in_kernel, k_scale=k_scale)
    return pl.pallas_call(
        kern,
        grid=(m // tm, n // tn),
        in_specs=[pl.BlockSpec((tm, k), lambda i, j: (i, 0)),
                  pl.BlockSpec((k, tn), lambda i, j: (0, j)),
                  _resident((k, LANES), lambda i, j: (0, 0))],
        out_specs=[pl.BlockSpec((tm, tn), lambda i, j: (i, j)),
                   pl.BlockSpec((tm, LANES), lambda i, j: (i, 0))],
        out_shape=[jax.ShapeDtypeStruct((m, n), BF16), jax.ShapeDtypeStruct((m, LANES), F32)],
        scratch_shapes=[pltpu.VMEM((tm, k), BF16)],
        compiler_params=_cparams("parallel", "arbitrary"),
        name="mlstm_in_proj",
    )(x, w_qkv, w_g)


def _log_sigmoid(x):
    return jnp.minimum(x, 0.0) - jnp.log(1.0 + jnp.exp(-jnp.abs(x)))


def _lane_scan(x, op, fill, backward):
    length = x.shape[1]
    lane = lax.broadcasted_iota(jnp.int32, x.shape, 1)
    k = 1
    while k < length:
        if backward:
            x = op(x, jnp.where(lane < length - k, pltpu.roll(x, length - k, 1), fill))
        else:
            x = op(x, jnp.where(lane >= k, pltpu.roll(x, k, 1), fill))
        k *= 2
    return x


def _gate_prework(gates, backward):
    nh = ML_HEADS
    length = gates.shape[1]
    f_rows = _log_sigmoid(gates[nh:])
    b_rows = _lane_scan(f_rows, jnp.add, 0.0, backward)
    a_rows = gates[:nh] - b_rows
    cm_rows = _lane_scan(a_rows, jnp.maximum, -jnp.inf, backward)
    b_tot = jnp.broadcast_to(jnp.sum(f_rows, axis=1, keepdims=True), (nh, length))
    a_max = jnp.broadcast_to(jnp.max(a_rows, axis=1, keepdims=True), (nh, length))
    rows = jnp.concatenate([a_rows, b_tot, a_max, jnp.zeros((nh, length), F32)], axis=0)
    cols = jnp.concatenate([cm_rows, b_rows, jnp.zeros((length - 2 * nh, length), F32)], axis=0).T
    return rows, cols


def _mlstm_direction(backward, qkv_ref, kt_ref, gn_ref, bias, o_ref, c_ref, m_ref, rows_ref, cols_ref,
                     dqk, dv):
    chunk = qkv_ref.shape[0]
    nh = ML_HEADS
    v0 = 2 * nh * dqk
    rows = rows_ref[...]
    cols = cols_ref[...]
    rows_next, cols_next = _gate_prework(gn_ref[0, 0] + bias, backward)
    rows_ref[...] = rows_next
    cols_ref[...] = cols_next

    row = lax.broadcasted_iota(jnp.int32, (chunk, chunk), 0)
    col = lax.broadcasted_iota(jnp.int32, (chunk, chunk), 1)
    mask = (col >= row) if backward else (col <= row)

    a_rows = rows[:nh]
    m_prev_all = m_ref[:, 0:1]
    m_x_all = jnp.maximum(m_prev_all, rows[2 * nh:3 * nh, 0:1])
    g_s_all = jnp.exp(m_prev_all - m_x_all)
    ws_rows = jnp.exp(a_rows - m_x_all)
    m_ref[...] = jnp.broadcast_to(rows[nh:2 * nh, 0:1] + m_x_all, m_ref.shape)
    ones = jnp.ones((chunk, LANES), BF16)

    for h in range(nh):
        m_prev = m_prev_all[h:h + 1, :]
        cm_col = jnp.broadcast_to(cols[:, h:h + 1], (chunk, LANES))
        b_col = jnp.broadcast_to(cols[:, nh + h:nh + h + 1], (chunk, LANES))
        m_col = jnp.maximum(m_prev, cm_col)
        floor = jnp.exp(-(b_col + m_col))
        w_mat = jnp.exp(jnp.where(mask, a_rows[h:h + 1, :], -jnp.inf) - m_col)
        g_col = jnp.exp(m_prev - m_col)

        qh = qkv_ref[:, h * dqk:(h + 1) * dqk]
        kth = kt_ref[h * dqk:(h + 1) * dqk, :]
        v_ext = jnp.concatenate([qkv_ref[:, v0 + h * dv:v0 + (h + 1) * dv], ones], axis=1)
        ct = c_ref[h]
        s = jnp.dot(qh, kth, preferred_element_type=F32) * w_mat
        inter = jnp.dot(qh, ct.astype(BF16), preferred_element_type=F32)
        intra = jnp.dot(s.astype(BF16), v_ext, preferred_element_type=F32)
        den = jnp.maximum(jnp.abs(g_col * inter[:, dv:] + intra[:, dv:]), floor)
        for part in range(dv // LANES):
            sl = slice(part * LANES, (part + 1) * LANES)
            o_ref[:, h * dv + part * LANES:h * dv + (part + 1) * LANES] = (
                (g_col * inter[:, sl] + intra[:, sl]) / den)

        kts = (kth.astype(F32) * ws_rows[h:h + 1, :]).astype(BF16)
        c_ref[h] = g_s_all[h:h + 1, :] * ct + jnp.dot(kts, v_ext, preferred_element_type=F32)


def _mlstm_kernel(qkvf_ref, ktf_ref, g0f_ref, gnf_ref, qkvb_ref, ktb_ref, g0b_ref, gnb_ref, bias_ref,
                  of_ref, ob_ref, c_ref, m_ref, rows_ref, cols_ref, *, dqk, dv):
    @pl.when(pl.program_id(1) == 0)
    def _():
        c_ref[...] = jnp.zeros_like(c_ref)
        m_ref[...] = jnp.zeros_like(m_ref)
        for d, g_ref in enumerate((g0f_ref, g0b_ref)):
            rows0, cols0 = _gate_prework(g_ref[0, 0] + bias_ref[d], d == 1)
            rows_ref[d] = rows0
            cols_ref[d] = cols0

    dirs = ((qkvf_ref, ktf_ref, gnf_ref, of_ref), (qkvb_ref, ktb_ref, gnb_ref, ob_ref))
    for d, (qkv_ref, kt_ref, gn_ref, o_ref) in enumerate(dirs):
        _mlstm_direction(d == 1, qkv_ref, kt_ref, gn_ref, bias_ref[d], o_ref, c_ref.at[d], m_ref.at[d],
                         rows_ref.at[d], cols_ref.at[d], dqk, dv)


def _mlstm(qkv, kt, gates_rows, bias_rows, batch, seq):
    t = qkv.shape[0]
    dv = qkv.shape[1] // (2 * ML_HEADS)
    dqk = dv // 2
    chunk = ML_CHUNK
    assert chunk == LANES and seq % chunk == 0
    nc = seq // chunk
    wqk = ML_HEADS * dqk
    wv = ML_HEADS * dv

    def specs(d):
        def cb(c):
            return nc - 1 - c if d else c

        def cn(c):
            return cb(jnp.minimum(c + 1, nc - 1))

        return [pl.BlockSpec((chunk, 2 * wqk + wv), lambda b, c: (b * nc + cb(c), 0)),
                pl.BlockSpec((wqk, chunk), lambda b, c: (b, cb(c))),
                pl.BlockSpec((1, 1, 2 * ML_HEADS, chunk), lambda b, c: (b, d, 0, cb(0))),
                pl.BlockSpec((1, 1, 2 * ML_HEADS, chunk), lambda b, c: (b, d, 0, cn(c)))]

    kern = functools.partial(_mlstm_kernel, dqk=dqk, dv=dv)
    return pl.pallas_call(
        kern,
        grid=(batch, nc),
        in_specs=specs(0) + specs(1) + [pl.BlockSpec((2, 2 * ML_HEADS, 1), lambda b, c: (0, 0, 0))],
        out_specs=[pl.BlockSpec((chunk, wv), lambda b, c: (b * nc + c, 0)),
                   pl.BlockSpec((chunk, wv), lambda b, c: (b * nc + nc - 1 - c, 0))],
        out_shape=[jax.ShapeDtypeStruct((t, wv), F32), jax.ShapeDtypeStruct((t, wv), F32)],
        scratch_shapes=[pltpu.VMEM((2, ML_HEADS, dqk, dv + LANES), F32),
                        pltpu.VMEM((2, ML_HEADS, LANES), F32),
                        pltpu.VMEM((2, 4 * ML_HEADS, chunk), F32),
                        pltpu.VMEM((2, chunk, chunk), F32)],
        compiler_params=_cparams("parallel", "arbitrary"),
        name="mlstm_chunks",
    )(qkv, kt, gates_rows, gates_rows, qkv, kt, gates_rows, gates_rows, bias_rows)


def _ml_out_kernel(hf_ref, hb_ref, x_ref, wo_ref, gain_ref, w_ref, g_ref, b_ref, o_ref, a_ref, *, dv):
    x = x_ref[...]
    og = jnp.dot(x.astype(BF16), wo_ref[...], preferred_element_type=F32)
    for h in range(ML_HEADS):
        sl = slice(h * dv, (h + 1) * dv)
        hs = hf_ref[:, sl] + hb_ref[:, sl]
        ms = jnp.mean(hs * hs, axis=-1, keepdims=True)
        hn = hs * lax.rsqrt(ms + RMS_EPS) * gain_ref[:, sl]
        a_ref[:, sl] = (hn * jax.nn.sigmoid(og[:, sl])).astype(BF16)
    y = jnp.dot(a_ref[...], w_ref[...], preferred_element_type=F32)
    o_ref[...] = _deepnorm_ln(x, y, g_ref[...], b_ref[...])


def _ml_out(h_fwd, h_bwd, x, w_ogate, gain, w, g, b, tm=256):
    m, d = h_fwd.shape
    tm = min(tm, m)
    kern = functools.partial(_ml_out_kernel, dv=d // ML_HEADS)
    row = lambda i: (i, 0)
    const = lambda i: (0, 0)
    return pl.pallas_call(
        kern,
        grid=(m // tm,),
        in_specs=[pl.BlockSpec((tm, d), row),
                  pl.BlockSpec((tm, d), row),
                  pl.BlockSpec((tm, d), row),
                  _resident((d, d), const),
                  _resident((1, d), const),
                  _resident((d, d), const),
                  _resident((1, d), const),
                  _resident((1, d), const)],
        out_specs=pl.BlockSpec((tm, d), row),
        out_shape=jax.ShapeDtypeStruct((m, d), F32),
        scratch_shapes=[pltpu.VMEM((tm, d), BF16)],
        compiler_params=_cparams("parallel"),
        name="mlstm_out_ln",
    )(h_fwd, h_bwd, x, w_ogate, gain.reshape(1, d), w, g.reshape(1, d), b.reshape(1, d))


def _xattn_fold_kernel(kv_ref, wq_ref, wo_ref, wqk_ref, vo_ref):
    mem_len = kv_ref.shape[0]
    d = wq_ref.shape[0]
    hd = d // XA_HEADS
    for h in range(XA_HEADS):
        hs = slice(h * hd, (h + 1) * hd)
        ms = slice(h * mem_len, (h + 1) * mem_len)
        wqk_ref[0, :, ms] = lax.dot_general(wq_ref[:, hs], kv_ref[:, hs], (((1,), (1,)), ((), ())),
                                            preferred_element_type=F32).astype(wqk_ref.dtype)
        vo_ref[0, ms, :] = jnp.dot(kv_ref[:, d + h * hd:d + (h + 1) * hd], wo_ref[hs, :],
                                   preferred_element_type=F32).astype(vo_ref.dtype)


def _xattn_fold(kv, wq, wo, batch):
    d = wq.shape[0]
    mem_len = kv.shape[0] // batch
    hm = XA_HEADS * mem_len
    return pl.pallas_call(
        _xattn_fold_kernel,
        grid=(batch,),
        in_specs=[pl.BlockSpec((mem_len, 2 * d), lambda bi: (bi, 0)),
                  _resident((d, d), lambda bi: (0, 0)),
                  _resident((d, d), lambda bi: (0, 0))],
        out_specs=[pl.BlockSpec((1, d, hm), lambda bi: (bi, 0, 0)),
                   pl.BlockSpec((1, hm, d), lambda bi: (bi, 0, 0))],
        out_shape=[jax.ShapeDtypeStruct((batch, d, hm), BF16),
                   jax.ShapeDtypeStruct((batch, hm, d), BF16)],
        compiler_params=_cparams("parallel"),
        name="xattn_fold",
    )(kv, wq, wo)


def _xattn_kernel(x_ref, wqk_ref, vo_ref, g_ref, b_ref, o_ref, p_ref, *, mem_len):
    x = x_ref[...]
    scale = (x.shape[1] // XA_HEADS) ** -0.5
    s = jnp.dot(x.astype(BF16), wqk_ref[0], preferred_element_type=F32) * scale
    for h in range(XA_HEADS):
        ms = slice(h * mem_len, (h + 1) * mem_len)
        sh = s[:, ms]
        e = jnp.exp(sh - jnp.max(sh, axis=-1, keepdims=True))
        p_ref[:, ms] = (e / jnp.sum(e, axis=-1, keepdims=True)).astype(BF16)
    y = jnp.dot(p_ref[...], vo_ref[0], preferred_element_type=F32)
    o_ref[...] = _deepnorm_ln(x, y, g_ref[...], b_ref[...])


def _xattn(x, wqk, vo, g, b, batch, seq, tq=512):
    t, d = x.shape
    hm = wqk.shape[2]
    tq = min(tq, seq)
    nq = seq // tq
    const = lambda bi, i: (0, 0)
    kern = functools.partial(_xattn_kernel, mem_len=hm // XA_HEADS)
    return pl.pallas_call(
        kern,
        grid=(batch, nq),
        in_specs=[pl.BlockSpec((tq, d), lambda bi, i: (bi * nq + i, 0)),
                  pl.BlockSpec((1, d, hm), lambda bi, i: (bi, 0, 0)),
                  pl.BlockSpec((1, hm, d), lambda bi, i: (bi, 0, 0)),
                  _resident((1, d), const),
                  _resident((1, d), const)],
        out_specs=pl.BlockSpec((tq, d), lambda bi, i: (bi * nq + i, 0)),
        out_shape=jax.ShapeDtypeStruct((t, d), F32),
        scratch_shapes=[pltpu.VMEM((tq, hm), BF16)],
        compiler_params=_cparams("parallel", "arbitrary"),
        name="xattn_sublayer",
    )(x, wqk, vo, g.reshape(1, d), b.reshape(1, d))


def _mlp_kernel(x_ref, w1_ref, w2_ref, g_ref, b_ref, o_ref, xb_ref):
    f = pl.program_id(1)

    @pl.when(f == 0)
    def _():
        xb_ref[...] = x_ref[...].astype(BF16)
        o_ref[...] = jnp.zeros_like(o_ref)

    h = jnp.dot(xb_ref[...], w1_ref[...], preferred_element_type=F32)
    h = jnp.maximum(h, 0.0)
    o_ref[...] += jnp.dot((h * h).astype(BF16), w2_ref[...], preferred_element_type=F32)

    @pl.when(f == pl.num_programs(1) - 1)
    def _():
        o_ref[...] = _deepnorm_ln(x_ref[...], o_ref[...], g_ref[...], b_ref[...])


def _mlp(x, w1, w2, g, b, tm=1024, tf=512):
    m, d = x.shape
    dff = w1.shape[1]
    tm = min(tm, m)
    tf = min(tf, dff)
    const = lambda i, f: (0, 0)
    return pl.pallas_call(
        _mlp_kernel,
        grid=(m // tm, dff // tf),
        in_specs=[pl.BlockSpec((tm, d), lambda i, f: (i, 0)),
                  pl.BlockSpec((d, tf), lambda i, f: (0, f)),
                  pl.BlockSpec((tf, d), lambda i, f: (f, 0)),
                  _resident((1, d), const),
                  _resident((1, d), const)],
        out_specs=pl.BlockSpec((tm, d), lambda i, f: (i, 0)),
        out_shape=jax.ShapeDtypeStruct((m, d), F32),
        scratch_shapes=[pltpu.VMEM((tm, d), BF16)],
        compiler_params=_cparams("parallel", "arbitrary"),
        name="mlp_sublayer",
    )(x, w1, w2, g.reshape(1, d), b.reshape(1, d))


def _gqa_sublayer(x, batch, seq, w_in, q_gain, k_gain, w_out, ln_g, ln_b):
    d = x.shape[1]
    tables = _rope_tables(seq)
    nkv = d // HEAD_DIM // GQA_GROUP
    kd = nkv * HEAD_DIM
    w = jnp.concatenate([_half_split_heads(w_in[:, :d + kd]), w_in[:, d + kd:]], axis=1).astype(BF16)
    q_scale = HEAD_DIM ** -0.5 * math.log2(math.e)
    q, kv = _attn_proj(x, w, q_gain, k_gain, tables, seq, q_scale)
    o = _flash_attention(q, kv, batch, seq)
    return _out_ln(o, w_out.astype(BF16), x, ln_g, ln_b)


def _mlstm_sublayer(x, batch, seq, w_in, b_gate, head_gain, w_out, ln_g, ln_b):
    t, d = x.shape
    dv = d // ML_HEADS
    dqk = dv // 2
    nqk = ML_HEADS * dqk
    w_qkv = w_in[:, :2 * nqk + d].astype(BF16)
    w_o = w_in[:, 2 * nqk + d:2 * nqk + 2 * d].astype(BF16)
    ng = 4 * ML_HEADS
    w_g = jnp.pad(w_in[:, 2 * nqk + 2 * d:], ((0, 0), (0, LANES - ng))).astype(BF16)
    qkv, gates = _ml_in_proj(x, w_qkv, w_g, dqk ** -0.5, nqk)
    gates = gates[:, :ng].reshape(batch, seq, 2, 2 * ML_HEADS).transpose(0, 2, 3, 1)
    kt = qkv[:, nqk:2 * nqk].reshape(batch, seq, nqk).transpose(0, 2, 1).reshape(batch * nqk, seq)
    bias = b_gate.astype(F32).reshape(2, 2 * ML_HEADS, 1)
    h_fwd, h_bwd = _mlstm(qkv, kt, gates, bias, batch, seq)
    return _ml_out(h_fwd, h_bwd, x, w_o, head_gain.reshape(-1), w_out.astype(BF16), ln_g, ln_b)


def _trunk(x3, mem3, p):
    batch, seq, d = x3.shape
    x = x3.reshape(batch * seq, d)
    mem = mem3.reshape(-1, d)
    for i in range(DEPTH):
        j = i // 2
        if i % 2 == 0:
            x = _gqa_sublayer(x, batch, seq, p['attn_w_in'][j], p['attn_q_gain'][j], p['attn_k_gain'][j],
                              p['attn_w_out'][j], p['ln_g'][i, 0], p['ln_b'][i, 0])
        else:
            x = _mlstm_sublayer(x, batch, seq, p['ml_w_in'][j], p['ml_b_gate'][j], p['ml_head_gain'][j],
                                p['ml_w_out'][j], p['ln_g'][i, 0], p['ln_b'][i, 0])
        kv = _proj(mem, p['xa_w_kv'][i].astype(BF16), BF16, _plain_epilogue, name="xattn_kv_proj")
        wqk, vo = _xattn_fold(kv, p['xa_w_q'][i].astype(BF16), p['xa_w_out'][i].astype(BF16), batch)
        x = _xattn(x, wqk, vo, p['ln_g'][i, 1], p['ln_b'][i, 1], batch, seq)
        x = _mlp(x, p['mlp_w1'][i].astype(BF16), p['mlp_w2'][i].astype(BF16),
                 p['ln_g'][i, 2], p['ln_b'][i, 2])
    return x.reshape(batch, seq, d)


def kernel(x_prompt, x_sample, mem_prompt, mem_sample, attn_w_in, attn_q_gain, attn_k_gain, attn_w_out,
           ml_w_in, ml_b_gate, ml_head_gain, ml_w_out, xa_w_q, xa_w_kv, xa_w_out, mlp_w1, mlp_w2,
           ln_g, ln_b):
    params = {
        'attn_w_in': attn_w_in, 'attn_q_gain': attn_q_gain, 'attn_k_gain': attn_k_gain,
        'attn_w_out': attn_w_out, 'ml_w_in': ml_w_in, 'ml_b_gate': ml_b_gate,
        'ml_head_gain': ml_head_gain, 'ml_w_out': ml_w_out, 'xa_w_q': xa_w_q, 'xa_w_kv': xa_w_kv,
        'xa_w_out': xa_w_out, 'mlp_w1': mlp_w1, 'mlp_w2': mlp_w2, 'ln_g': ln_g, 'ln_b': ln_b,
    }
    y_prompt = _trunk(x_prompt, mem_prompt, params)
    y_sample = _trunk(x_sample, mem_sample, params)
    return (y_prompt, y_sample)
```

```python
import functools
import math

import jax
import jax.numpy as jnp
from jax import lax
from jax.experimental import pallas as pl
from jax.experimental.pallas import tpu as pltpu

F32 = jnp.float32
BF16 = jnp.bfloat16

DEPTH = 4
HEAD_DIM = 128
GQA_GROUP = 4
GRID_W = 64
ROPE_THETA = 10000.0
ML_HEADS = 8
XA_HEADS = 4
DN_ALPHA = (2 * DEPTH) ** 0.25
LN_EPS = 1e-5
RMS_EPS = 1e-6

V7X_VMEM_BYTES = 64 * 1024 * 1024
VMEM_LIMIT_BYTES = V7X_VMEM_BYTES - 4 * 1024 * 1024
LANES = 128

ML_CHUNK = 128
LN_ROWS = 128


def _cparams(*sem):
    return pltpu.CompilerParams(dimension_semantics=sem, vmem_limit_bytes=VMEM_LIMIT_BYTES)


def _resident(block_shape, index_map):
    return pl.BlockSpec(block_shape, index_map, pipeline_mode=pl.Buffered(1))


def _deepnorm_ln(res, y, g, b):
    z = DN_ALPHA * res + y
    mu = jnp.mean(z, axis=-1, keepdims=True)
    zc = z - mu
    var = jnp.mean(zc * zc, axis=-1, keepdims=True)
    return zc * lax.rsqrt(var + LN_EPS) * g + b


def _proj_kernel(x_ref, w_ref, *rest, n_extra, epilogue):
    extra = rest[:n_extra]
    o_ref = rest[n_extra]
    xb_ref = rest[n_extra + 1]

    @pl.when(pl.program_id(1) == 0)
    def _():
        xb_ref[...] = x_ref[...].astype(BF16)

    acc = jnp.dot(xb_ref[...], w_ref[...], preferred_element_type=F32)
    epilogue(acc, extra, o_ref)


def _proj(x, w, out_dtype, epilogue, extras=(), extra_specs=(), tm=1024, tn=1024, name="proj"):
    m, k = x.shape
    n = w.shape[1]
    tm = min(tm, m)
    tn = min(tn, n)
    kern = functools.partial(_proj_kernel, n_extra=len(extras), epilogue=epilogue)
    return pl.pallas_call(
        kern,
        grid=(m // tm, n // tn),
        in_specs=[pl.BlockSpec((tm, k), lambda i, j: (i, 0)),
                  pl.BlockSpec((k, tn), lambda i, j: (0, j))] + list(extra_specs),
        out_specs=pl.BlockSpec((tm, tn), lambda i, j: (i, j)),
        out_shape=jax.ShapeDtypeStruct((m, n), out_dtype),
        scratch_shapes=[pltpu.VMEM((tm, k), BF16)],
        compiler_params=_cparams("parallel", "arbitrary"),
        name=name,
    )(x, w, *extras)


def _plain_epilogue(acc, extra, o_ref):
    o_ref[...] = acc.astype(o_ref.dtype)


def _rope_tables(s):
    rows = s // GRID_W
    row_ids = jnp.repeat(jnp.arange(rows), GRID_W).astype(F32)
    col_ids = jnp.tile(jnp.arange(GRID_W), rows).astype(F32)
    axis_dim = HEAD_DIM // 2
    inv_freq = ROPE_THETA ** (-jnp.arange(0, axis_dim, 2, dtype=F32) / axis_dim)
    ang = jnp.concatenate([row_ids[:, None] * inv_freq, col_ids[:, None] * inv_freq], axis=1)
    cos = jnp.concatenate([jnp.cos(ang), jnp.cos(ang)], axis=1)
    sin = jnp.concatenate([-jnp.sin(ang), jnp.sin(ang)], axis=1)
    return cos, sin


def _half_split_heads(a):
    lead = a.shape[:-1]
    a = a.reshape(*lead, -1, 2, 2, HEAD_DIM // 4)
    return jnp.swapaxes(a, -3, -2).reshape(*lead, -1)


def _attn_proj_kernel(x_ref, w_ref, cos_ref, sin_ref, gq_ref, gk_ref, q_ref, kv_ref, xb_ref, acc_ref, *,
                      q_scale, rows):
    j = pl.program_id(1)
    last = pl.num_programs(1) - 1

    @pl.when(j == 0)
    def _():
        xb_ref[...] = x_ref[...].astype(BF16)

    acc_ref[...] = jnp.dot(xb_ref[...], w_ref[...], preferred_element_type=F32)
    tm, tn = acc_ref.shape
    ones = jnp.ones((HEAD_DIM, HEAD_DIM), BF16)

    def norm_rope(o_ref, heads, gains_ref, out_scale):
        g = gains_ref[0:1, :]
        gsw = gains_ref[1:2, :]

        def body(r, carry):
            r0 = pl.multiple_of(r * rows, rows)
            gc = cos_ref[pl.ds(r0, rows), :] * g
            gs = sin_ref[pl.ds(r0, rows), :] * gsw
            for h in range(heads):
                sl = slice(h * HEAD_DIM, (h + 1) * HEAD_DIM)
                a = acc_ref[pl.ds(r0, rows), sl]
                ssq = jnp.dot((a * a).astype(BF16), ones, preferred_element_type=F32)
                rr = lax.rsqrt(ssq + HEAD_DIM * RMS_EPS) * (out_scale * HEAD_DIM ** 0.5)
                y = (a * gc + pltpu.roll(a, HEAD_DIM // 2, 1) * gs) * rr
                o_ref[pl.ds(r0, rows), sl] = y.astype(o_ref.dtype)
            return carry

        lax.fori_loop(0, tm // rows, body, 0, unroll=2)

    @pl.when(j < last)
    def _():
        norm_rope(q_ref, tn // HEAD_DIM, gq_ref, q_scale)

    @pl.when(j == last)
    def _():
        k_heads = tn // (2 * HEAD_DIM)
        norm_rope(kv_ref, k_heads, gk_ref, 1.0)
        k_cols = k_heads * HEAD_DIM
        for h in range(k_heads):
            src = slice(k_cols + h * HEAD_DIM, k_cols + (h + 1) * HEAD_DIM)
            dst = k_cols + 2 * h * HEAD_DIM
            kv_ref[:, dst:dst + HEAD_DIM] = acc_ref[:, src].astype(kv_ref.dtype)
            kv_ref[:, dst + HEAD_DIM:dst + 2 * HEAD_DIM] = jnp.ones((tm, HEAD_DIM), kv_ref.dtype)


def _attn_proj(x, w, q_gain, k_gain, tables, seq, q_scale, tm=1024, rows=128):
    m, k = x.shape
    tn = w.shape[1] - k
    tm = min(tm, seq)
    rows = min(rows, tm)
    nblk = seq // tm
    nq = k // tn
    cos, sin = tables

    def gains(gain):
        gain = _half_split_heads(gain).reshape(1, HEAD_DIM)
        return jnp.concatenate([gain, jnp.roll(gain, HEAD_DIM // 2, axis=1)], axis=0)

    kern = functools.partial(_attn_proj_kernel, q_scale=q_scale, rows=rows)
    tab_spec = pl.BlockSpec((tm, HEAD_DIM), lambda i, j: (i % nblk, 0))
    g_spec = pl.BlockSpec((2, HEAD_DIM), lambda i, j: (0, 0))
    return pl.pallas_call(
        kern,
        grid=(m // tm, nq + 1),
        in_specs=[pl.BlockSpec((tm, k), lambda i, j: (i, 0)),
                  pl.BlockSpec((k, tn), lambda i, j: (0, j)),
                  tab_spec, tab_spec, g_spec, g_spec],
        out_specs=[pl.BlockSpec((tm, tn), lambda i, j: (i, jnp.minimum(j, nq - 1))),
                   pl.BlockSpec((tm, 3 * tn // 2), lambda i, j: (i, 0))],
        out_shape=[jax.ShapeDtypeStruct((m, k), BF16),
                   jax.ShapeDtypeStruct((m, 3 * tn // 2), BF16)],
        scratch_shapes=[pltpu.VMEM((tm, k), BF16), pltpu.VMEM((tm, tn), F32)],
        compiler_params=_cparams("parallel", "arbitrary"),
        name="attn_qkv_proj",
    )(x, w, cos, sin, gains(q_gain), gains(k_gain))


def _flash_kernel(q_ref, k_ref, v_ref, o_ref, q4_ref, *, tq, tk):
    seq = k_ref.shape[0]
    for h in range(GQA_GROUP):
        q4_ref[h * tq:(h + 1) * tq, :] = q_ref[:, h * HEAD_DIM:(h + 1) * HEAD_DIM]
    q4 = q4_ref[...]
    rows = GQA_GROUP * tq

    m = jnp.full((rows, 1), -jnp.inf, F32)
    acc = jnp.zeros((rows, 2 * HEAD_DIM), F32)
    for c in range(seq // tk):
        kc = k_ref[c * tk:(c + 1) * tk, :]
        vc = v_ref[c * tk:(c + 1) * tk, :]
        s = lax.dot_general(q4, kc, (((1,), (1,)), ((), ())), preferred_element_type=F32)
        m_new = jnp.maximum(m, jnp.max(s, axis=-1, keepdims=True))
        p = jnp.exp2(s - m_new).astype(BF16)
        acc = jnp.exp2(m - m_new) * acc + jnp.dot(p, vc, preferred_element_type=F32)
        m = m_new
    out = acc[:, :HEAD_DIM] / acc[:, HEAD_DIM:]
    for h in range(GQA_GROUP):
        o_ref[:, h * HEAD_DIM:(h + 1) * HEAD_DIM] = out[h * tq:(h + 1) * tq, :].astype(o_ref.dtype)


def _flash_attention(q, kv, batch, seq, tq=512, tk=256):
    t, dq = q.shape
    nkv = dq // HEAD_DIM // GQA_GROUP
    tq = min(tq, seq)
    tk = min(tk, seq)
    nq = seq // tq
    gw = GQA_GROUP * HEAD_DIM
    v0 = nkv // 2
    kern = functools.partial(_flash_kernel, tq=tq, tk=tk)
    return pl.pallas_call(
        kern,
        grid=(batch, nkv, nq),
        in_specs=[pl.BlockSpec((tq, gw), lambda b, g, i: (b * nq + i, g)),
                  pl.BlockSpec((seq, HEAD_DIM), lambda b, g, i: (b, g)),
                  pl.BlockSpec((seq, 2 * HEAD_DIM), lambda b, g, i: (b, v0 + g))],
        out_specs=pl.BlockSpec((tq, gw), lambda b, g, i: (b * nq + i, g)),
        out_shape=jax.ShapeDtypeStruct((t, dq), BF16),
        scratch_shapes=[pltpu.VMEM((GQA_GROUP * tq, HEAD_DIM), BF16)],
        compiler_params=_cparams("parallel", "parallel", "arbitrary"),
        name="flash_attention",
    )(q, kv, kv)


def _out_ln_kernel(a_ref, w_ref, res_ref, g_ref, b_ref, o_ref):
    y = jnp.dot(a_ref[...], w_ref[...], preferred_element_type=F32)
    o_ref[...] = _deepnorm_ln(res_ref[...], y, g_ref[...], b_ref[...])


def _out_ln(a, w, res, g, b, tm=512):
    m, k = a.shape
    d = w.shape[1]
    tm = min(tm, m)
    return pl.pallas_call(
        _out_ln_kernel,
        grid=(m // tm,),
        in_specs=[pl.BlockSpec((tm, k), lambda i: (i, 0)),
                  _resident((k, d), lambda i: (0, 0)),
                  pl.BlockSpec((tm, d), lambda i: (i, 0)),
                  _resident((1, d), lambda i: (0, 0)),
                  _resident((1, d), lambda i: (0, 0))],
        out_specs=pl.BlockSpec((tm, d), lambda i: (i, 0)),
        out_shape=jax.ShapeDtypeStruct((m, d), F32),
        compiler_params=_cparams("parallel"),
        name="attn_out_ln",
    )(a, w, res, g.reshape(1, d), b.reshape(1, d))


def _ml_in_kernel(x_ref, w_ref, wg_ref, qkv_ref, gates_ref, xb_ref, *, k_scale):
    j = pl.program_id(1)

    @pl.when(j == 0)
    def _():
        xb_ref[...] = x_ref[...].astype(BF16)
        gates_ref[...] = jnp.dot(xb_ref[...], wg_ref[...], preferred_element_type=F32)

    acc = jnp.dot(xb_ref[...], w_ref[...], preferred_element_type=F32)
    scale = jnp.where(j == 1, k_scale, 1.0).astype(F32)
    qkv_ref[...] = (acc * scale).astype(qkv_ref.dtype)


def _ml_in_proj(x, w_qkv, w_g, k_scale, tn, tm=1024):
    m, k = x.shape
    n = w_qkv.shape[1]
    tm = min(tm, m)
    kern = functools.partial(_ml_in_kernel, k_scale=k_scale)
    return pl.pallas_call(
        kern,
        grid=(m // tm, n // tn),
        in_specs=[pl.BlockSpec((tm, k), lambda i, j: (i, 0)),
                  pl.BlockSpec((k, tn), lambda i, j: (0, j)),
                  _resident((k, LANES), lambda i, j: (0, 0))],
        out_specs=[pl.BlockSpec((tm, tn), lambda i, j: (i, j)),
                   pl.BlockSpec((tm, LANES), lambda i, j: (i, 0))],
        out_shape=[jax.ShapeDtypeStruct((m, n), BF16), jax.ShapeDtypeStruct((m, LANES), F32)],
        scratch_shapes=[pltpu.VMEM((tm, k), BF16)],
        compiler_params=_cparams("parallel", "arbitrary"),
        name="mlstm_in_proj",
    )(x, w_qkv, w_g)


def _log_sigmoid(x):
    return jnp.minimum(x, 0.0) - jnp.log(1.0 + jnp.exp(-jnp.abs(x)))


def _lane_scan(x, op, fill, backward):
    length = x.shape[1]
    lane = lax.broadcasted_iota(jnp.int32, x.shape, 1)
    k = 1
    while k < length:
        if backward:
            x = op(x, jnp.where(lane < length - k, pltpu.roll(x, length - k, 1), fill))
        else:
            x = op(x, jnp.where(lane >= k, pltpu.roll(x, k, 1), fill))
        k *= 2
    return x


def _gate_prework(gates, backward):
    nh = ML_HEADS
    length = gates.shape[1]
    f_rows = _log_sigmoid(gates[nh:])
    b_rows = _lane_scan(f_rows, jnp.add, 0.0, backward)
    a_rows = gates[:nh] - b_rows
    cm_rows = _lane_scan(a_rows, jnp.maximum, -jnp.inf, backward)
    b_tot = jnp.broadcast_to(jnp.sum(f_rows, axis=1, keepdims=True), (nh, length))
    a_max = jnp.broadcast_to(jnp.max(a_rows, axis=1, keepdims=True), (nh, length))
    rows = jnp.concatenate([a_rows, b_tot, a_max, jnp.zeros((nh, length), F32)], axis=0)
    cols = jnp.concatenate([cm_rows, b_rows, jnp.zeros((length - 2 * nh, length), F32)], axis=0).T
    return rows, cols


def _mlstm_direction(backward, qkv_ref, kt_ref, gn_ref, bias, o_ref, c_ref, m_ref, rows_ref, cols_ref,
                     dqk, dv):
    chunk = qkv_ref.shape[0]
    nh = ML_HEADS
    v0 = 2 * nh * dqk
    rows = rows_ref[...]
    cols = cols_ref[...]
    rows_next, cols_next = _gate_prework(gn_ref[0, 0] + bias, backward)
    rows_ref[...] = rows_next
    cols_ref[...] = cols_next

    row = lax.broadcasted_iota(jnp.int32, (chunk, chunk), 0)
    col = lax.broadcasted_iota(jnp.int32, (chunk, chunk), 1)
    mask = (col >= row) if backward else (col <= row)

    a_rows = rows[:nh]
    m_prev_all = m_ref[:, 0:1]
    m_x_all = jnp.maximum(m_prev_all, rows[2 * nh:3 * nh, 0:1])
    g_s_all = jnp.exp(m_prev_all - m_x_all)
    ws_rows = jnp.exp(a_rows - m_x_all)
    m_ref[...] = jnp.broadcast_to(rows[nh:2 * nh, 0:1] + m_x_all, m_ref.shape)
    ones = jnp.ones((chunk, LANES), BF16)

    for h in range(nh):
        m_prev = m_prev_all[h:h + 1, :]
        cm_col = jnp.broadcast_to(cols[:, h:h + 1], (chunk, LANES))
        b_col = jnp.broadcast_to(cols[:, nh + h:nh + h + 1], (chunk, LANES))
        m_col = jnp.maximum(m_prev, cm_col)
        floor = jnp.exp(-(b_col + m_col))
        w_mat = jnp.exp(jnp.where(mask, a_rows[h:h + 1, :], -jnp.inf) - m_col)
        g_col = jnp.exp(m_prev - m_col)

        qh = qkv_ref[:, h * dqk:(h + 1) * dqk]
        kth = kt_ref[h * dqk:(h + 1) * dqk, :]
        v_ext = jnp.concatenate([qkv_ref[:, v0 + h * dv:v0 + (h + 1) * dv], ones], axis=1)
        ct = c_ref[h]
        s = jnp.dot(qh, kth, preferred_element_type=F32) * w_mat
        inter = jnp.dot(qh, ct.astype(BF16), preferred_element_type=F32)
        intra = jnp.dot(s.astype(BF16), v_ext, preferred_element_type=F32)
        den = jnp.maximum(jnp.abs(g_col * inter[:, dv:] + intra[:, dv:]), floor)
        for part in range(dv // LANES):
            sl = slice(part * LANES, (part + 1) * LANES)
            o_ref[:, h * dv + part * LANES:h * dv + (part + 1) * LANES] = (
                (g_col * inter[:, sl] + intra[:, sl]) / den)

        kts = (kth.astype(F32) * ws_rows[h:h + 1, :]).astype(BF16)
        c_ref[h] = g_s_all[h:h + 1, :] * ct + jnp.dot(kts, v_ext, preferred_element_type=F32)


def _mlstm_kernel(qkvf_ref, ktf_ref, g0f_ref, gnf_ref, qkvb_ref, ktb_ref, g0b_ref, gnb_ref, bias_ref,
                  of_ref, ob_ref, c_ref, m_ref, rows_ref, cols_ref, *, dqk, dv):
    @pl.when(pl.program_id(1) == 0)
    def _():
        c_ref[...] = jnp.zeros_like(c_ref)
        m_ref[...] = jnp.zeros_like(m_ref)
        for d, g_ref in enumerate((g0f_ref, g0b_ref)):
            rows0, cols0 = _gate_prework(g_ref[0, 0] + bias_ref[d], d == 1)
            rows_ref[d] = rows0
            cols_ref[d] = cols0

    dirs = ((qkvf_ref, ktf_ref, gnf_ref, of_ref), (qkvb_ref, ktb_ref, gnb_ref, ob_ref))
    for d, (qkv_ref, kt_ref, gn_ref, o_ref) in enumerate(dirs):
        _mlstm_direction(d == 1, qkv_ref, kt_ref, gn_ref, bias_ref[d], o_ref, c_ref.at[d], m_ref.at[d],
                         rows_ref.at[d], cols_ref.at[d], dqk, dv)


def _mlstm(qkv, kt, gates_rows, bias_rows, batch, seq):
    t = qkv.shape[0]
    dv = qkv.shape[1] // (2 * ML_HEADS)
    dqk = dv // 2
    chunk = ML_CHUNK
    assert chunk == LANES and seq % chunk == 0
    nc = seq // chunk
    wqk = ML_HEADS * dqk
    wv = ML_HEADS * dv

    def specs(d):
        def cb(c):
            return nc - 1 - c if d else c

        def cn(c):
            return cb(jnp.minimum(c + 1, nc - 1))

        return [pl.BlockSpec((chunk, 2 * wqk + wv), lambda b, c: (b * nc + cb(c), 0)),
                pl.BlockSpec((wqk, chunk), lambda b, c: (b, cb(c))),
                pl.BlockSpec((1, 1, 2 * ML_HEADS, chunk), lambda b, c: (b, d, 0, cb(0))),
                pl.BlockSpec((1, 1, 2 * ML_HEADS, chunk), lambda b, c: (b, d, 0, cn(c)))]

    kern = functools.partial(_mlstm_kernel, dqk=dqk, dv=dv)
    return pl.pallas_call(
        kern,
        grid=(batch, nc),
        in_specs=specs(0) + specs(1) + [pl.BlockSpec((2, 2 * ML_HEADS, 1), lambda b, c: (0, 0, 0))],
        out_specs=[pl.BlockSpec((chunk, wv), lambda b, c: (b * nc + c, 0)),
                   pl.BlockSpec((chunk, wv), lambda b, c: (b * nc + nc - 1 - c, 0))],
        out_shape=[jax.ShapeDtypeStruct((t, wv), F32), jax.ShapeDtypeStruct((t, wv), F32)],
        scratch_shapes=[pltpu.VMEM((2, ML_HEADS, dqk, dv + LANES), F32),
                        pltpu.VMEM((2, ML_HEADS, LANES), F32),
                        pltpu.VMEM((2, 4 * ML_HEADS, chunk), F32),
                        pltpu.VMEM((2, chunk, chunk), F32)],
        compiler_params=_cparams("parallel", "arbitrary"),
        name="mlstm_chunks",
    )(qkv, kt, gates_rows, gates_rows, qkv, kt, gates_rows, gates_rows, bias_rows)


def _ml_out_kernel(hf_ref, hb_ref, x_ref, wo_ref, gain_ref, w_ref, g_ref, b_ref, o_ref, a_ref, *, dv):
    x = x_ref[...]
    og = jnp.dot(x.astype(BF16), wo_ref[...], preferred_element_type=F32)
    for h in range(ML_HEADS):
        sl = slice(h * dv, (h + 1) * dv)
        hs = hf_ref[:, sl] + hb_ref[:, sl]
        ms = jnp.mean(hs * hs, axis=-1, keepdims=True)
        hn = hs * lax.rsqrt(ms + RMS_EPS) * gain_ref[:, sl]
        a_ref[:, sl] = (hn * jax.nn.sigmoid(og[:, sl])).astype(BF16)
    y = jnp.dot(a_ref[...], w_ref[...], preferred_element_type=F32)
    o_ref[...] = _deepnorm_ln(x, y, g_ref[...], b_ref[...])


def _ml_out(h_fwd, h_bwd, x, w_ogate, gain, w, g, b, tm=256):
    m, d = h_fwd.shape
    tm = min(tm, m)
    kern = functools.partial(_ml_out_kernel, dv=d // ML_HEADS)
    row = lambda i: (i, 0)
    const = lambda i: (0, 0)
    return pl.pallas_call(
        kern,
        grid=(m // tm,),
        in_specs=[pl.BlockSpec((tm, d), row),
                  pl.BlockSpec((tm, d), row),
                  pl.BlockSpec((tm, d), row),
                  _resident((d, d), const),
                  _resident((1, d), const),
                  _resident((d, d), const),
                  _resident((1, d), const),
                  _resident((1, d), const)],
        out_specs=pl.BlockSpec((tm, d), row),
        out_shape=jax.ShapeDtypeStruct((m, d), F32),
        scratch_shapes=[pltpu.VMEM((tm, d), BF16)],
        compiler_params=_cparams("parallel"),
        name="mlstm_out_ln",
    )(h_fwd, h_bwd, x, w_ogate, gain.reshape(1, d), w, g.reshape(1, d), b.reshape(1, d))


def _xattn_fold_kernel(kv_ref, wq_ref, wo_ref, wqk_ref, vo_ref):
    mem_len = kv_ref.shape[0]
    d = wq_ref.shape[0]
    hd = d // XA_HEADS
    for h in range(XA_HEADS):
        hs = slice(h * hd, (h + 1) * hd)
        ms = slice(h * mem_len, (h + 1) * mem_len)
        wqk_ref[0, :, ms] = lax.dot_general(wq_ref[:, hs], kv_ref[:, hs], (((1,), (1,)), ((), ())),
                                            preferred_element_type=F32).astype(wqk_ref.dtype)
        vo_ref[0, ms, :] = jnp.dot(kv_ref[:, d + h * hd:d + (h + 1) * hd], wo_ref[hs, :],
                                   preferred_element_type=F32).astype(vo_ref.dtype)


def _xattn_fold(kv, wq, wo, batch):
    d = wq.shape[0]
    mem_len = kv.shape[0] // batch
    hm = XA_HEADS * mem_len
    return pl.pallas_call(
        _xattn_fold_kernel,
        grid=(batch,),
        in_specs=[pl.BlockSpec((mem_len, 2 * d), lambda bi: (bi, 0)),
                  _resident((d, d), lambda bi: (0, 0)),
                  _resident((d, d), lambda bi: (0, 0))],
        out_specs=[pl.BlockSpec((1, d, hm), lambda bi: (bi, 0, 0)),
                   pl.BlockSpec((1, hm, d), lambda bi: (bi, 0, 0))],
        out_shape=[jax.ShapeDtypeStruct((batch, d, hm), BF16),
                   jax.ShapeDtypeStruct((batch, hm, d), BF16)],
        compiler_params=_cparams("parallel"),
        name="xattn_fold",
    )(kv, wq, wo)


def _xattn_kernel(x_ref, wqk_ref, vo_ref, g_ref, b_ref, o_ref, p_ref, *, mem_len):
    x = x_ref[...]
    scale = (x.shape[1] // XA_HEADS) ** -0.5
    s = jnp.dot(x.astype(BF16), wqk_ref[0], preferred_element_type=F32) * scale
    for h in range(XA_HEADS):
        ms = slice(h * mem_len, (h + 1) * mem_len)
        sh = s[:, ms]
        e = jnp.exp(sh - jnp.max(sh, axis=-1, keepdims=True))
        p_ref[:, ms] = (e / jnp.sum(e, axis=-1, keepdims=True)).astype(BF16)
    y = jnp.dot(p_ref[...], vo_ref[0], preferred_element_type=F32)
    o_ref[...] = _deepnorm_ln(x, y, g_ref[...], b_ref[...])


def _xattn(x, wqk, vo, g, b, batch, seq, tq=1024):
    t, d = x.shape
    hm = wqk.shape[2]
    tq = min(tq, seq)
    nq = seq // tq
    const = lambda bi, i: (0, 0)
    kern = functools.partial(_xattn_kernel, mem_len=hm // XA_HEADS)
    return pl.pallas_call(
        kern,
        grid=(batch, nq),
        in_specs=[pl.BlockSpec((tq, d), lambda bi, i: (bi * nq + i, 0)),
                  pl.BlockSpec((1, d, hm), lambda bi, i: (bi, 0, 0)),
                  pl.BlockSpec((1, hm, d), lambda bi, i: (bi, 0, 0)),
                  _resident((1, d), const),
                  _resident((1, d), const)],
        out_specs=pl.BlockSpec((tq, d), lambda bi, i: (bi * nq + i, 0)),
        out_shape=jax.ShapeDtypeStruct((t, d), F32),
        scratch_shapes=[pltpu.VMEM((tq, hm), BF16)],
        compiler_params=_cparams("parallel", "arbitrary"),
        name="xattn_sublayer",
    )(x, wqk, vo, g.reshape(1, d), b.reshape(1, d))


def _mlp_kernel(x_ref, w1_ref, w2_ref, g_ref, b_ref, o_ref, xb_ref):
    f = pl.program_id(1)

    @pl.when(f == 0)
    def _():
        xb_ref[...] = x_ref[...].astype(BF16)
        o_ref[...] = jnp.zeros_like(o_ref)

    h = jnp.dot(xb_ref[...], w1_ref[...], preferred_element_type=F32)
    h = jnp.maximum(h, 0.0)
    o_ref[...] += jnp.dot((h * h).astype(BF16), w2_ref[...], preferred_element_type=F32)

    @pl.when(f == pl.num_programs(1) - 1)
    def _():
        rows = min(LN_ROWS, o_ref.shape[0])

        def body(r, carry):
            rs = pl.ds(pl.multiple_of(r * rows, rows), rows)
            o_ref[rs, :] = _deepnorm_ln(x_ref[rs, :], o_ref[rs, :], g_ref[...], b_ref[...])
            return carry

        lax.fori_loop(0, o_ref.shape[0] // rows, body, 0)


def _mlp(x, w1, w2, g, b, tm=1024, tf=1024):
    m, d = x.shape
    dff = w1.shape[1]
    tm = min(tm, m)
    tf = min(tf, dff)
    const = lambda i, f: (0, 0)
    return pl.pallas_call(
        _mlp_kernel,
        grid=(m // tm, dff // tf),
        in_specs=[pl.BlockSpec((tm, d), lambda i, f: (i, 0)),
                  pl.BlockSpec((d, tf), lambda i, f: (0, f)),
                  pl.BlockSpec((tf, d), lambda i, f: (f, 0)),
                  _resident((1, d), const),
                  _resident((1, d), const)],
        out_specs=pl.BlockSpec((tm, d), lambda i, f: (i, 0)),
        out_shape=jax.ShapeDtypeStruct((m, d), F32),
        scratch_shapes=[pltpu.VMEM((tm, d), BF16)],
        compiler_params=_cparams("parallel", "arbitrary"),
        name="mlp_sublayer",
    )(x, w1, w2, g.reshape(1, d), b.reshape(1, d))


def _gqa_sublayer(x, batch, seq, w_in, q_gain, k_gain, w_out, ln_g, ln_b):
    d = x.shape[1]
    tables = _rope_tables(seq)
    nkv = d // HEAD_DIM // GQA_GROUP
    kd = nkv * HEAD_DIM
    w = jnp.concatenate([_half_split_heads(w_in[:, :d + kd]), w_in[:, d + kd:]], axis=1).astype(BF16)
    q_scale = HEAD_DIM ** -0.5 * math.log2(math.e)
    q, kv = _attn_proj(x, w, q_gain, k_gain, tables, seq, q_scale)
    o = _flash_attention(q, kv, batch, seq)
    return _out_ln(o, w_out.astype(BF16), x, ln_g, ln_b)


def _mlstm_sublayer(x, batch, seq, w_in, b_gate, head_gain, w_out, ln_g, ln_b):
    t, d = x.shape
    dv = d // ML_HEADS
    dqk = dv // 2
    nqk = ML_HEADS * dqk
    w_qkv = w_in[:, :2 * nqk + d].astype(BF16)
    w_o = w_in[:, 2 * nqk + d:2 * nqk + 2 * d].astype(BF16)
    ng = 4 * ML_HEADS
    w_g = jnp.pad(w_in[:, 2 * nqk + 2 * d:], ((0, 0), (0, LANES - ng))).astype(BF16)
    qkv, gates = _ml_in_proj(x, w_qkv, w_g, dqk ** -0.5, nqk)
    gates = gates[:, :ng].reshape(batch, seq, 2, 2 * ML_HEADS).transpose(0, 2, 3, 1)
    kt = qkv[:, nqk:2 * nqk].reshape(batch, seq, nqk).transpose(0, 2, 1).reshape(batch * nqk, seq)
    bias = b_gate.astype(F32).reshape(2, 2 * ML_HEADS, 1)
    h_fwd, h_bwd = _mlstm(qkv, kt, gates, bias, batch, seq)
    return _ml_out(h_fwd, h_bwd, x, w_o, head_gain.reshape(-1), w_out.astype(BF16), ln_g, ln_b)


def _trunk(x3, mem3, p):
    batch, seq, d = x3.shape
    x = x3.reshape(batch * seq, d)
    mem = mem3.reshape(-1, d)
    for i in range(DEPTH):
        j = i // 2
        if i % 2 == 0:
            x = _gqa_sublayer(x, batch, seq, p['attn_w_in'][j], p['attn_q_gain'][j], p['attn_k_gain'][j],
                              p['attn_w_out'][j], p['ln_g'][i, 0], p['ln_b'][i, 0])
        else:
            x = _mlstm_sublayer(x, batch, seq, p['ml_w_in'][j], p['ml_b_gate'][j], p['ml_head_gain'][j],
                                p['ml_w_out'][j], p['ln_g'][i, 0], p['ln_b'][i, 0])
        kv = _proj(mem, p['xa_w_kv'][i].astype(BF16), BF16, _plain_epilogue, name="xattn_kv_proj")
        wqk, vo = _xattn_fold(kv, p['xa_w_q'][i].astype(BF16), p['xa_w_out'][i].astype(BF16), batch)
        x = _xattn(x, wqk, vo, p['ln_g'][i, 1], p['ln_b'][i, 1], batch, seq)
        x = _mlp(x, p['mlp_w1'][i].astype(BF16), p['mlp_w2'][i].astype(BF16),
                 p['ln_g'][i, 2], p['ln_b'][i, 2])
    return x.reshape(batch, seq, d)


def kernel(x_prompt, x_sample, mem_prompt, mem_sample, attn_w_in, attn_q_gain, attn_k_gain, attn_w_out,
           ml_w_in, ml_b_gate, ml_head_gain, ml_w_out, xa_w_q, xa_w_kv, xa_w_out, mlp_w1, mlp_w2,
           ln_g, ln_b):
    params = {
        'attn_w_in': attn_w_in, 'attn_q_gain': attn_q_gain, 'attn_k_gain': attn_k_gain,
        'attn_w_out': attn_w_out, 'ml_w_in': ml_w_in, 'ml_b_gate': ml_b_gate,
        'ml_head_gain': ml_head_gain, 'ml_w_out': ml_w_out, 'xa_w_q': xa_w_q, 'xa_w_kv': xa_w_kv,
        'xa_w_out': xa_w_out, 'mlp_w1': mlp_w1, 'mlp_w2': mlp_w2, 'ln_g': ln_g, 'ln_b': ln_b,
    }
    y_prompt = _trunk(x_prompt, mem_prompt, params)
    y_sample = _trunk(x_sample, mem_sample, params)
    return (y_prompt, y_sample)
```

```python
import functools
import math

import jax
import jax.numpy as jnp
from jax import lax
from jax.experimental import pallas as pl
from jax.experimental.pallas import tpu as pltpu

F32 = jnp.float32
BF16 = jnp.bfloat16

DEPTH = 4
HEAD_DIM = 128
GQA_GROUP = 4
GRID_W = 64
ROPE_THETA = 10000.0
ML_HEADS = 8
XA_HEADS = 4
DN_ALPHA = (2 * DEPTH) ** 0.25
LN_EPS = 1e-5
RMS_EPS = 1e-6

V7X_VMEM_BYTES = 64 * 1024 * 1024
VMEM_LIMIT_BYTES = V7X_VMEM_BYTES - 4 * 1024 * 1024
LANES = 128

ML_CHUNK = 128
LN_ROWS = 128


def _cparams(*sem):
    return pltpu.CompilerParams(dimension_semantics=sem, vmem_limit_bytes=VMEM_LIMIT_BYTES)


def _resident(block_shape, index_map):
    return pl.BlockSpec(block_shape, index_map, pipeline_mode=pl.Buffered(1))


def _deepnorm_ln(res, y, g, b):
    z = DN_ALPHA * res + y
    mu = jnp.mean(z, axis=-1, keepdims=True)
    zc = z - mu
    var = jnp.mean(zc * zc, axis=-1, keepdims=True)
    return zc * lax.rsqrt(var + LN_EPS) * g + b


def _proj_kernel(x_ref, w_ref, *rest, n_extra, epilogue):
    extra = rest[:n_extra]
    o_ref = rest[n_extra]
    xb_ref = rest[n_extra + 1]

    @pl.when(pl.program_id(1) == 0)
    def _():
        xb_ref[...] = x_ref[...].astype(BF16)

    acc = jnp.dot(xb_ref[...], w_ref[...], preferred_element_type=F32)
    epilogue(acc, extra, o_ref)


def _proj(x, w, out_dtype, epilogue, extras=(), extra_specs=(), tm=1024, tn=1024, name="proj"):
    m, k = x.shape
    n = w.shape[1]
    tm = min(tm, m)
    tn = min(tn, n)
    kern = functools.partial(_proj_kernel, n_extra=len(extras), epilogue=epilogue)
    return pl.pallas_call(
        kern,
        grid=(m // tm, n // tn),
        in_specs=[pl.BlockSpec((tm, k), lambda i, j: (i, 0)),
                  pl.BlockSpec((k, tn), lambda i, j: (0, j))] + list(extra_specs),
        out_specs=pl.BlockSpec((tm, tn), lambda i, j: (i, j)),
        out_shape=jax.ShapeDtypeStruct((m, n), out_dtype),
        scratch_shapes=[pltpu.VMEM((tm, k), BF16)],
        compiler_params=_cparams("parallel", "arbitrary"),
        name=name,
    )(x, w, *extras)


def _plain_epilogue(acc, extra, o_ref):
    o_ref[...] = acc.astype(o_ref.dtype)


def _rope_tables(s):
    rows = s // GRID_W
    row_ids = jnp.repeat(jnp.arange(rows), GRID_W).astype(F32)
    col_ids = jnp.tile(jnp.arange(GRID_W), rows).astype(F32)
    axis_dim = HEAD_DIM // 2
    inv_freq = ROPE_THETA ** (-jnp.arange(0, axis_dim, 2, dtype=F32) / axis_dim)
    ang = jnp.concatenate([row_ids[:, None] * inv_freq, col_ids[:, None] * inv_freq], axis=1)
    cos = jnp.concatenate([jnp.cos(ang), jnp.cos(ang)], axis=1)
    sin = jnp.concatenate([-jnp.sin(ang), jnp.sin(ang)], axis=1)
    return cos, sin


def _half_split_heads(a):
    lead = a.shape[:-1]
    a = a.reshape(*lead, -1, 2, 2, HEAD_DIM // 4)
    return jnp.swapaxes(a, -3, -2).reshape(*lead, -1)


def _attn_proj_kernel(x_ref, w_ref, cos_ref, sin_ref, gq_ref, gk_ref, q_ref, kv_ref, xb_ref, acc_ref, *,
                      q_scale, rows):
    j = pl.program_id(1)
    last = pl.num_programs(1) - 1

    @pl.when(j == 0)
    def _():
        xb_ref[...] = x_ref[...].astype(BF16)

    acc_ref[...] = jnp.dot(xb_ref[...], w_ref[...], preferred_element_type=F32)
    tm, tn = acc_ref.shape
    ones = jnp.ones((HEAD_DIM, HEAD_DIM), BF16)

    def norm_rope(o_ref, heads, gains_ref, out_scale):
        g = gains_ref[0:1, :]
        gsw = gains_ref[1:2, :]

        def body(r, carry):
            r0 = pl.multiple_of(r * rows, rows)
            gc = cos_ref[pl.ds(r0, rows), :] * g
            gs = sin_ref[pl.ds(r0, rows), :] * gsw
            for h in range(heads):
                sl = slice(h * HEAD_DIM, (h + 1) * HEAD_DIM)
                a = acc_ref[pl.ds(r0, rows), sl]
                ssq = jnp.dot((a * a).astype(BF16), ones, preferred_element_type=F32)
                rr = lax.rsqrt(ssq + HEAD_DIM * RMS_EPS) * (out_scale * HEAD_DIM ** 0.5)
                y = (a * gc + pltpu.roll(a, HEAD_DIM // 2, 1) * gs) * rr
                o_ref[pl.ds(r0, rows), sl] = y.astype(o_ref.dtype)
            return carry

        lax.fori_loop(0, tm // rows, body, 0, unroll=2)

    @pl.when(j < last)
    def _():
        norm_rope(q_ref, tn // HEAD_DIM, gq_ref, q_scale)

    @pl.when(j == last)
    def _():
        k_heads = tn // (2 * HEAD_DIM)
        norm_rope(kv_ref, k_heads, gk_ref, 1.0)
        k_cols = k_heads * HEAD_DIM
        for h in range(k_heads):
            src = slice(k_cols + h * HEAD_DIM, k_cols + (h + 1) * HEAD_DIM)
            dst = k_cols + 2 * h * HEAD_DIM
            kv_ref[:, dst:dst + HEAD_DIM] = acc_ref[:, src].astype(kv_ref.dtype)
            kv_ref[:, dst + HEAD_DIM:dst + 2 * HEAD_DIM] = jnp.ones((tm, HEAD_DIM), kv_ref.dtype)


def _attn_proj(x, w, q_gain, k_gain, tables, seq, q_scale, tm=1024, rows=128):
    m, k = x.shape
    tn = w.shape[1] - k
    tm = min(tm, seq)
    rows = min(rows, tm)
    nblk = seq // tm
    nq = k // tn
    cos, sin = tables

    def gains(gain):
        gain = _half_split_heads(gain).reshape(1, HEAD_DIM)
        return jnp.concatenate([gain, jnp.roll(gain, HEAD_DIM // 2, axis=1)], axis=0)

    kern = functools.partial(_attn_proj_kernel, q_scale=q_scale, rows=rows)
    tab_spec = pl.BlockSpec((tm, HEAD_DIM), lambda i, j: (i % nblk, 0))
    g_spec = pl.BlockSpec((2, HEAD_DIM), lambda i, j: (0, 0))
    return pl.pallas_call(
        kern,
        grid=(m // tm, nq + 1),
        in_specs=[pl.BlockSpec((tm, k), lambda i, j: (i, 0)),
                  pl.BlockSpec((k, tn), lambda i, j: (0, j)),
                  tab_spec, tab_spec, g_spec, g_spec],
        out_specs=[pl.BlockSpec((tm, tn), lambda i, j: (i, jnp.minimum(j, nq - 1))),
                   pl.BlockSpec((tm, 3 * tn // 2), lambda i, j: (i, 0))],
        out_shape=[jax.ShapeDtypeStruct((m, k), BF16),
                   jax.ShapeDtypeStruct((m, 3 * tn // 2), BF16)],
        scratch_shapes=[pltpu.VMEM((tm, k), BF16), pltpu.VMEM((tm, tn), F32)],
        compiler_params=_cparams("parallel", "arbitrary"),
        name="attn_qkv_proj",
    )(x, w, cos, sin, gains(q_gain), gains(k_gain))


def _flash_kernel(q_ref, k_ref, v_ref, o_ref, q4_ref, *, tq, tk):
    seq = k_ref.shape[0]
    for h in range(GQA_GROUP):
        q4_ref[h * tq:(h + 1) * tq, :] = q_ref[:, h * HEAD_DIM:(h + 1) * HEAD_DIM]
    q4 = q4_ref[...]
    rows = GQA_GROUP * tq

    m = jnp.full((rows, 1), -jnp.inf, F32)
    acc = jnp.zeros((rows, 2 * HEAD_DIM), F32)
    for c in range(seq // tk):
        kc = k_ref[c * tk:(c + 1) * tk, :]
        vc = v_ref[c * tk:(c + 1) * tk, :]
        s = lax.dot_general(q4, kc, (((1,), (1,)), ((), ())), preferred_element_type=F32)
        m_new = jnp.maximum(m, jnp.max(s, axis=-1, keepdims=True))
        p = jnp.exp2(s - m_new).astype(BF16)
        acc = jnp.exp2(m - m_new) * acc + jnp.dot(p, vc, preferred_element_type=F32)
        m = m_new
    out = acc[:, :HEAD_DIM] / acc[:, HEAD_DIM:]
    for h in range(GQA_GROUP):
        o_ref[:, h * HEAD_DIM:(h + 1) * HEAD_DIM] = out[h * tq:(h + 1) * tq, :].astype(o_ref.dtype)


def _flash_attention(q, kv, batch, seq, tq=1024, tk=256):
    t, dq = q.shape
    nkv = dq // HEAD_DIM // GQA_GROUP
    tq = min(tq, seq)
    tk = min(tk, seq)
    nq = seq // tq
    gw = GQA_GROUP * HEAD_DIM
    v0 = nkv // 2
    kern = functools.partial(_flash_kernel, tq=tq, tk=tk)
    return pl.pallas_call(
        kern,
        grid=(batch, nkv, nq),
        in_specs=[pl.BlockSpec((tq, gw), lambda b, g, i: (b * nq + i, g)),
                  pl.BlockSpec((seq, HEAD_DIM), lambda b, g, i: (b, g)),
                  pl.BlockSpec((seq, 2 * HEAD_DIM), lambda b, g, i: (b, v0 + g))],
        out_specs=pl.BlockSpec((tq, gw), lambda b, g, i: (b * nq + i, g)),
        out_shape=jax.ShapeDtypeStruct((t, dq), BF16),
        scratch_shapes=[pltpu.VMEM((GQA_GROUP * tq, HEAD_DIM), BF16)],
        compiler_params=_cparams("parallel", "parallel", "arbitrary"),
        name="flash_attention",
    )(q, kv, kv)


def _out_ln_kernel(a_ref, w_ref, res_ref, g_ref, b_ref, o_ref):
    y = jnp.dot(a_ref[...], w_ref[...], preferred_element_type=F32)
    o_ref[...] = _deepnorm_ln(res_ref[...], y, g_ref[...], b_ref[...])


def _out_ln(a, w, res, g, b, tm=512):
    m, k = a.shape
    d = w.shape[1]
    tm = min(tm, m)
    return pl.pallas_call(
        _out_ln_kernel,
        grid=(m // tm,),
        in_specs=[pl.BlockSpec((tm, k), lambda i: (i, 0)),
                  _resident((k, d), lambda i: (0, 0)),
                  pl.BlockSpec((tm, d), lambda i: (i, 0)),
                  _resident((1, d), lambda i: (0, 0)),
                  _resident((1, d), lambda i: (0, 0))],
        out_specs=pl.BlockSpec((tm, d), lambda i: (i, 0)),
        out_shape=jax.ShapeDtypeStruct((m, d), F32),
        compiler_params=_cparams("parallel"),
        name="attn_out_ln",
    )(a, w, res, g.reshape(1, d), b.reshape(1, d))


def _ml_in_kernel(x_ref, w_ref, wg_ref, qkv_ref, gates_ref, xb_ref, *, k_scale):
    j = pl.program_id(1)

    @pl.when(j == 0)
    def _():
        xb_ref[...] = x_ref[...].astype(BF16)
        gates_ref[...] = jnp.dot(xb_ref[...], wg_ref[...], preferred_element_type=F32)

    acc = jnp.dot(xb_ref[...], w_ref[...], preferred_element_type=F32)
    scale = jnp.where(j == 1, k_scale, 1.0).astype(F32)
    qkv_ref[...] = (acc * scale).astype(qkv_ref.dtype)


def _ml_in_proj(x, w_qkv, w_g, k_scale, tn, tm=1024):
    m, k = x.shape
    n = w_qkv.shape[1]
    tm = min(tm, m)
    kern = functools.partial(_ml_in_kernel, k_scale=k_scale)
    return pl.pallas_call(
        kern,
        grid=(m // tm, n // tn),
        in_specs=[pl.BlockSpec((tm, k), lambda i, j: (i, 0)),
                  pl.BlockSpec((k, tn), lambda i, j: (0, j)),
                  _resident((k, LANES), lambda i, j: (0, 0))],
        out_specs=[pl.BlockSpec((tm, tn), lambda i, j: (i, j)),
                   pl.BlockSpec((tm, LANES), lambda i, j: (i, 0))],
        out_shape=[jax.ShapeDtypeStruct((m, n), BF16), jax.ShapeDtypeStruct((m, LANES), F32)],
        scratch_shapes=[pltpu.VMEM((tm, k), BF16)],
        compiler_params=_cparams("parallel", "arbitrary"),
        name="mlstm_in_proj",
    )(x, w_qkv, w_g)


def _log_sigmoid(x):
    return jnp.minimum(x, 0.0) - jnp.log(1.0 + jnp.exp(-jnp.abs(x)))


def _lane_scan(x, op, fill, backward):
    length = x.shape[1]
    lane = lax.broadcasted_iota(jnp.int32, x.shape, 1)
    k = 1
    while k < length:
        if backward:
            x = op(x, jnp.where(lane < length - k, pltpu.roll(x, length - k, 1), fill))
        else:
            x = op(x, jnp.where(lane >= k, pltpu.roll(x, k, 1), fill))
        k *= 2
    return x


def _gate_prework(gates, backward):
    nh = ML_HEADS
    length = gates.shape[1]
    f_rows = _log_sigmoid(gates[nh:])
    b_rows = _lane_scan(f_rows, jnp.add, 0.0, backward)
    a_rows = gates[:nh] - b_rows
    cm_rows = _lane_scan(a_rows, jnp.maximum, -jnp.inf, backward)
    b_tot = jnp.broadcast_to(jnp.sum(f_rows, axis=1, keepdims=True), (nh, length))
    a_max = jnp.broadcast_to(jnp.max(a_rows, axis=1, keepdims=True), (nh, length))
    rows = jnp.concatenate([a_rows, b_tot, a_max, jnp.zeros((nh, length), F32)], axis=0)
    cols = jnp.concatenate([cm_rows, b_rows, jnp.zeros((length - 2 * nh, length), F32)], axis=0).T
    return rows, cols


def _mlstm_direction(backward, qkv_ref, kt_ref, gn_ref, bias, o_ref, c_ref, m_ref, rows_ref, cols_ref,
                     dqk, dv):
    chunk = qkv_ref.shape[0]
    nh = ML_HEADS
    v0 = 2 * nh * dqk
    rows = rows_ref[...]
    cols = cols_ref[...]
    rows_next, cols_next = _gate_prework(gn_ref[0, 0] + bias, backward)
    rows_ref[...] = rows_next
    cols_ref[...] = cols_next

    row = lax.broadcasted_iota(jnp.int32, (chunk, chunk), 0)
    col = lax.broadcasted_iota(jnp.int32, (chunk, chunk), 1)
    mask = (col >= row) if backward else (col <= row)

    a_rows = rows[:nh]
    m_prev_all = m_ref[:, 0:1]
    m_x_all = jnp.maximum(m_prev_all, rows[2 * nh:3 * nh, 0:1])
    g_s_all = jnp.exp(m_prev_all - m_x_all)
    ws_rows = jnp.exp(a_rows - m_x_all)
    m_ref[...] = jnp.broadcast_to(rows[nh:2 * nh, 0:1] + m_x_all, m_ref.shape)
    ones = jnp.ones((chunk, LANES), BF16)

    for h in range(nh):
        m_prev = m_prev_all[h:h + 1, :]
        cm_col = jnp.broadcast_to(cols[:, h:h + 1], (chunk, LANES))
        b_col = jnp.broadcast_to(cols[:, nh + h:nh + h + 1], (chunk, LANES))
        m_col = jnp.maximum(m_prev, cm_col)
        floor = jnp.exp(-(b_col + m_col))
        w_mat = jnp.exp(jnp.where(mask, a_rows[h:h + 1, :], -jnp.inf) - m_col)
        g_col = jnp.exp(m_prev - m_col)

        qh = qkv_ref[:, h * dqk:(h + 1) * dqk]
        kth = kt_ref[h * dqk:(h + 1) * dqk, :]
        v_ext = jnp.concatenate([qkv_ref[:, v0 + h * dv:v0 + (h + 1) * dv], ones], axis=1)
        ct = c_ref[h]
        s = jnp.dot(qh, kth, preferred_element_type=F32) * w_mat
        inter = jnp.dot(qh, ct.astype(BF16), preferred_element_type=F32)
        intra = jnp.dot(s.astype(BF16), v_ext, preferred_element_type=F32)
        den = jnp.maximum(jnp.abs(g_col * inter[:, dv:] + intra[:, dv:]), floor)
        for part in range(dv // LANES):
            sl = slice(part * LANES, (part + 1) * LANES)
            o_ref[:, h * dv + part * LANES:h * dv + (part + 1) * LANES] = (
                (g_col * inter[:, sl] + intra[:, sl]) / den)

        kts = (kth.astype(F32) * ws_rows[h:h + 1, :]).astype(BF16)
        c_ref[h] = g_s_all[h:h + 1, :] * ct + jnp.dot(kts, v_ext, preferred_element_type=F32)


def _mlstm_kernel(qkvf_ref, ktf_ref, g0f_ref, gnf_ref, qkvb_ref, ktb_ref, g0b_ref, gnb_ref, bias_ref,
                  of_ref, ob_ref, c_ref, m_ref, rows_ref, cols_ref, *, dqk, dv):
    @pl.when(pl.program_id(1) == 0)
    def _():
        c_ref[...] = jnp.zeros_like(c_ref)
        m_ref[...] = jnp.zeros_like(m_ref)
        for d, g_ref in enumerate((g0f_ref, g0b_ref)):
            rows0, cols0 = _gate_prework(g_ref[0, 0] + bias_ref[d], d == 1)
            rows_ref[d] = rows0
            cols_ref[d] = cols0

    dirs = ((qkvf_ref, ktf_ref, gnf_ref, of_ref), (qkvb_ref, ktb_ref, gnb_ref, ob_ref))
    for d, (qkv_ref, kt_ref, gn_ref, o_ref) in enumerate(dirs):
        _mlstm_direction(d == 1, qkv_ref, kt_ref, gn_ref, bias_ref[d], o_ref, c_ref.at[d], m_ref.at[d],
                         rows_ref.at[d], cols_ref.at[d], dqk, dv)


def _mlstm(qkv, kt, gates_rows, bias_rows, batch, seq):
    t = qkv.shape[0]
    dv = qkv.shape[1] // (2 * ML_HEADS)
    dqk = dv // 2
    chunk = ML_CHUNK
    assert chunk == LANES and seq % chunk == 0
    nc = seq // chunk
    wqk = ML_HEADS * dqk
    wv = ML_HEADS * dv

    def specs(d):
        def cb(c):
            return nc - 1 - c if d else c

        def cn(c):
            return cb(jnp.minimum(c + 1, nc - 1))

        return [pl.BlockSpec((chunk, 2 * wqk + wv), lambda b, c: (b * nc + cb(c), 0)),
                pl.BlockSpec((wqk, chunk), lambda b, c: (b, cb(c))),
                pl.BlockSpec((1, 1, 2 * ML_HEADS, chunk), lambda b, c: (b, d, 0, cb(0))),
                pl.BlockSpec((1, 1, 2 * ML_HEADS, chunk), lambda b, c: (b, d, 0, cn(c)))]

    kern = functools.partial(_mlstm_kernel, dqk=dqk, dv=dv)
    return pl.pallas_call(
        kern,
        grid=(batch, nc),
        in_specs=specs(0) + specs(1) + [pl.BlockSpec((2, 2 * ML_HEADS, 1), lambda b, c: (0, 0, 0))],
        out_specs=[pl.BlockSpec((chunk, wv), lambda b, c: (b * nc + c, 0)),
                   pl.BlockSpec((chunk, wv), lambda b, c: (b * nc + nc - 1 - c, 0))],
        out_shape=[jax.ShapeDtypeStruct((t, wv), F32), jax.ShapeDtypeStruct((t, wv), F32)],
        scratch_shapes=[pltpu.VMEM((2, ML_HEADS, dqk, dv + LANES), F32),
                        pltpu.VMEM((2, ML_HEADS, LANES), F32),
                        pltpu.VMEM((2, 4 * ML_HEADS, chunk), F32),
                        pltpu.VMEM((2, chunk, chunk), F32)],
        compiler_params=_cparams("parallel", "arbitrary"),
        name="mlstm_chunks",
    )(qkv, kt, gates_rows, gates_rows, qkv, kt, gates_rows, gates_rows, bias_rows)


def _ml_out_kernel(hf_ref, hb_ref, x_ref, wo_ref, gain_ref, w_ref, g_ref, b_ref, o_ref, a_ref, *, dv):
    x = x_ref[...]
    og = jnp.dot(x.astype(BF16), wo_ref[...], preferred_element_type=F32)
    for h in range(ML_HEADS):
        sl = slice(h * dv, (h + 1) * dv)
        hs = hf_ref[:, sl] + hb_ref[:, sl]
        ms = jnp.mean(hs * hs, axis=-1, keepdims=True)
        hn = hs * lax.rsqrt(ms + RMS_EPS) * gain_ref[:, sl]
        a_ref[:, sl] = (hn * jax.nn.sigmoid(og[:, sl])).astype(BF16)
    y = jnp.dot(a_ref[...], w_ref[...], preferred_element_type=F32)
    o_ref[...] = _deepnorm_ln(x, y, g_ref[...], b_ref[...])


def _ml_out(h_fwd, h_bwd, x, w_ogate, gain, w, g, b, tm=512):
    m, d = h_fwd.shape
    tm = min(tm, m)
    kern = functools.partial(_ml_out_kernel, dv=d // ML_HEADS)
    row = lambda i: (i, 0)
    const = lambda i: (0, 0)
    return pl.pallas_call(
        kern,
        grid=(m // tm,),
        in_specs=[pl.BlockSpec((tm, d), row),
                  pl.BlockSpec((tm, d), row),
                  pl.BlockSpec((tm, d), row),
                  _resident((d, d), const),
                  _resident((1, d), const),
                  _resident((d, d), const),
                  _resident((1, d), const),
                  _resident((1, d), const)],
        out_specs=pl.BlockSpec((tm, d), row),
        out_shape=jax.ShapeDtypeStruct((m, d), F32),
        scratch_shapes=[pltpu.VMEM((tm, d), BF16)],
        compiler_params=_cparams("parallel"),
        name="mlstm_out_ln",
    )(h_fwd, h_bwd, x, w_ogate, gain.reshape(1, d), w, g.reshape(1, d), b.reshape(1, d))


def _xattn_fold_kernel(kv_ref, wq_ref, wo_ref, wqk_ref, vo_ref):
    mem_len = kv_ref.shape[0]
    d = wq_ref.shape[0]
    hd = d // XA_HEADS
    for h in range(XA_HEADS):
        hs = slice(h * hd, (h + 1) * hd)
        ms = slice(h * mem_len, (h + 1) * mem_len)
        wqk_ref[0, :, ms] = lax.dot_general(wq_ref[:, hs], kv_ref[:, hs], (((1,), (1,)), ((), ())),
                                            preferred_element_type=F32).astype(wqk_ref.dtype)
        vo_ref[0, ms, :] = jnp.dot(kv_ref[:, d + h * hd:d + (h + 1) * hd], wo_ref[hs, :],
                                   preferred_element_type=F32).astype(vo_ref.dtype)


def _xattn_fold(kv, wq, wo, batch):
    d = wq.shape[0]
    mem_len = kv.shape[0] // batch
    hm = XA_HEADS * mem_len
    return pl.pallas_call(
        _xattn_fold_kernel,
        grid=(batch,),
        in_specs=[pl.BlockSpec((mem_len, 2 * d), lambda bi: (bi, 0)),
                  _resident((d, d), lambda bi: (0, 0)),
                  _resident((d, d), lambda bi: (0, 0))],
        out_specs=[pl.BlockSpec((1, d, hm), lambda bi: (bi, 0, 0)),
                   pl.BlockSpec((1, hm, d), lambda bi: (bi, 0, 0))],
        out_shape=[jax.ShapeDtypeStruct((batch, d, hm), BF16),
                   jax.ShapeDtypeStruct((batch, hm, d), BF16)],
        compiler_params=_cparams("parallel"),
        name="xattn_fold",
    )(kv, wq, wo)


def _xattn_kernel(x_ref, wqk_ref, vo_ref, g_ref, b_ref, o_ref, p_ref, *, mem_len):
    x = x_ref[...]
    scale = (x.shape[1] // XA_HEADS) ** -0.5
    s = jnp.dot(x.astype(BF16), wqk_ref[0], preferred_element_type=F32) * scale
    for h in range(XA_HEADS):
        ms = slice(h * mem_len, (h + 1) * mem_len)
        sh = s[:, ms]
        e = jnp.exp(sh - jnp.max(sh, axis=-1, keepdims=True))
        p_ref[:, ms] = (e / jnp.sum(e, axis=-1, keepdims=True)).astype(BF16)
    y = jnp.dot(p_ref[...], vo_ref[0], preferred_element_type=F32)
    o_ref[...] = _deepnorm_ln(x, y, g_ref[...], b_ref[...])


def _xattn(x, wqk, vo, g, b, batch, seq, tq=1024):
    t, d = x.shape
    hm = wqk.shape[2]
    tq = min(tq, seq)
    nq = seq // tq
    const = lambda bi, i: (0, 0)
    kern = functools.partial(_xattn_kernel, mem_len=hm // XA_HEADS)
    return pl.pallas_call(
        kern,
        grid=(batch, nq),
        in_specs=[pl.BlockSpec((tq, d), lambda bi, i: (bi * nq + i, 0)),
                  pl.BlockSpec((1, d, hm), lambda bi, i: (bi, 0, 0)),
                  pl.BlockSpec((1, hm, d), lambda bi, i: (bi, 0, 0)),
                  _resident((1, d), const),
                  _resident((1, d), const)],
        out_specs=pl.BlockSpec((tq, d), lambda bi, i: (bi * nq + i, 0)),
        out_shape=jax.ShapeDtypeStruct((t, d), F32),
        scratch_shapes=[pltpu.VMEM((tq, hm), BF16)],
        compiler_params=_cparams("parallel", "arbitrary"),
        name="xattn_sublayer",
    )(x, wqk, vo, g.reshape(1, d), b.reshape(1, d))


def _mlp_kernel(x_ref, w1_ref, w2_ref, g_ref, b_ref, o_ref, xb_ref):
    f = pl.program_id(1)

    def partial_sum():
        h = jnp.dot(xb_ref[...], w1_ref[...], preferred_element_type=F32)
        h = jnp.maximum(h, 0.0)
        return jnp.dot((h * h).astype(BF16), w2_ref[...], preferred_element_type=F32)

    @pl.when(f == 0)
    def _():
        xb_ref[...] = x_ref[...].astype(BF16)
        o_ref[...] = partial_sum()

    @pl.when(f != 0)
    def _():
        o_ref[...] += partial_sum()

    @pl.when(f == pl.num_programs(1) - 1)
    def _():
        rows = min(LN_ROWS, o_ref.shape[0])

        def body(r, carry):
            rs = pl.ds(pl.multiple_of(r * rows, rows), rows)
            o_ref[rs, :] = _deepnorm_ln(x_ref[rs, :], o_ref[rs, :], g_ref[...], b_ref[...])
            return carry

        lax.fori_loop(0, o_ref.shape[0] // rows, body, 0)


def _mlp(x, w1, w2, g, b, tm=1024, tf=1024):
    m, d = x.shape
    dff = w1.shape[1]
    tm = min(tm, m)
    tf = min(tf, dff)
    const = lambda i, f: (0, 0)
    return pl.pallas_call(
        _mlp_kernel,
        grid=(m // tm, dff // tf),
        in_specs=[pl.BlockSpec((tm, d), lambda i, f: (i, 0)),
                  pl.BlockSpec((d, tf), lambda i, f: (0, f)),
                  pl.BlockSpec((tf, d), lambda i, f: (f, 0)),
                  _resident((1, d), const),
                  _resident((1, d), const)],
        out_specs=pl.BlockSpec((tm, d), lambda i, f: (i, 0)),
        out_shape=jax.ShapeDtypeStruct((m, d), F32),
        scratch_shapes=[pltpu.VMEM((tm, d), BF16)],
        compiler_params=_cparams("parallel", "arbitrary"),
        name="mlp_sublayer",
    )(x, w1, w2, g.reshape(1, d), b.reshape(1, d))


def _gqa_sublayer(x, batch, seq, w_in, q_gain, k_gain, w_out, ln_g, ln_b):
    d = x.shape[1]
    tables = _rope_tables(seq)
    nkv = d // HEAD_DIM // GQA_GROUP
    kd = nkv * HEAD_DIM
    w = jnp.concatenate([_half_split_heads(w_in[:, :d + kd]), w_in[:, d + kd:]], axis=1).astype(BF16)
    q_scale = HEAD_DIM ** -0.5 * math.log2(math.e)
    q, kv = _attn_proj(x, w, q_gain, k_gain, tables, seq, q_scale)
    o = _flash_attention(q, kv, batch, seq)
    return _out_ln(o, w_out.astype(BF16), x, ln_g, ln_b)


def _mlstm_sublayer(x, batch, seq, w_in, b_gate, head_gain, w_out, ln_g, ln_b):
    t, d = x.shape
    dv = d // ML_HEADS
    dqk = dv // 2
    nqk = ML_HEADS * dqk
    w_qkv = w_in[:, :2 * nqk + d].astype(BF16)
    w_o = w_in[:, 2 * nqk + d:2 * nqk + 2 * d].astype(BF16)
    ng = 4 * ML_HEADS
    w_g = jnp.pad(w_in[:, 2 * nqk + 2 * d:], ((0, 0), (0, LANES - ng))).astype(BF16)
    qkv, gates = _ml_in_proj(x, w_qkv, w_g, dqk ** -0.5, nqk)
    gates = gates[:, :ng].reshape(batch, seq, 2, 2 * ML_HEADS).transpose(0, 2, 3, 1)
    kt = qkv[:, nqk:2 * nqk].reshape(batch, seq, nqk).transpose(0, 2, 1).reshape(batch * nqk, seq)
    bias = b_gate.astype(F32).reshape(2, 2 * ML_HEADS, 1)
    h_fwd, h_bwd = _mlstm(qkv, kt, gates, bias, batch, seq)
    return _ml_out(h_fwd, h_bwd, x, w_o, head_gain.reshape(-1), w_out.astype(BF16), ln_g, ln_b)


def _trunk(x3, mem3, p):
    batch, seq, d = x3.shape
    x = x3.reshape(batch * seq, d)
    mem = mem3.reshape(-1, d)
    for i in range(DEPTH):
        j = i // 2
        if i % 2 == 0:
            x = _gqa_sublayer(x, batch, seq, p['attn_w_in'][j], p['attn_q_gain'][j], p['attn_k_gain'][j],
                              p['attn_w_out'][j], p['ln_g'][i, 0], p['ln_b'][i, 0])
        else:
            x = _mlstm_sublayer(x, batch, seq, p['ml_w_in'][j], p['ml_b_gate'][j], p['ml_head_gain'][j],
                                p['ml_w_out'][j], p['ln_g'][i, 0], p['ln_b'][i, 0])
        kv = _proj(mem, p['xa_w_kv'][i].astype(BF16), BF16, _plain_epilogue, name="xattn_kv_proj")
        wqk, vo = _xattn_fold(kv, p['xa_w_q'][i].astype(BF16), p['xa_w_out'][i].astype(BF16), batch)
        x = _xattn(x, wqk, vo, p['ln_g'][i, 1], p['ln_b'][i, 1], batch, seq)
        x = _mlp(x, p['mlp_w1'][i].astype(BF16), p['mlp_w2'][i].astype(BF16),
                 p['ln_g'][i, 2], p['ln_b'][i, 2])
    return x.reshape(batch, seq, d)


def kernel(x_prompt, x_sample, mem_prompt, mem_sample, attn_w_in, attn_q_gain, attn_k_gain, attn_w_out,
           ml_w_in, ml_b_gate, ml_head_gain, ml_w_out, xa_w_q, xa_w_kv, xa_w_out, mlp_w1, mlp_w2,
           ln_g, ln_b):
    params = {
        'attn_w_in': attn_w_in, 'attn_q_gain': attn_q_gain, 'attn_k_gain': attn_k_gain,
        'attn_w_out': attn_w_out, 'ml_w_in': ml_w_in, 'ml_b_gate': ml_b_gate,
        'ml_head_gain': ml_head_gain, 'ml_w_out': ml_w_out, 'xa_w_q': xa_w_q, 'xa_w_kv': xa_w_kv,
        'xa_w_out': xa_w_out, 'mlp_w1': mlp_w1, 'mlp_w2': mlp_w2, 'ln_g': ln_g, 'ln_b': ln_b,
    }
    y_prompt = _trunk(x_prompt, mem_prompt, params)
    y_sample = _trunk(x_sample, mem_sample, params)
    return (y_prompt, y_sample)
```

```python
import functools
import math

import jax
import jax.numpy as jnp
from jax import lax
from jax.experimental import pallas as pl
from jax.experimental.pallas import tpu as pltpu

F32 = jnp.float32
BF16 = jnp.bfloat16

DEPTH = 4
HEAD_DIM = 128
GQA_GROUP = 4
GRID_W = 64
ROPE_THETA = 10000.0
ML_HEADS = 8
XA_HEADS = 4
DN_ALPHA = (2 * DEPTH) ** 0.25
LN_EPS = 1e-5
RMS_EPS = 1e-6

V7X_VMEM_BYTES = 64 * 1024 * 1024
VMEM_LIMIT_BYTES = V7X_VMEM_BYTES - 4 * 1024 * 1024
LANES = 128

ML_CHUNK = 128
LN_ROWS = 128


def _cparams(*sem):
    return pltpu.CompilerParams(dimension_semantics=sem, vmem_limit_bytes=VMEM_LIMIT_BYTES)


def _resident(block_shape, index_map):
    return pl.BlockSpec(block_shape, index_map, pipeline_mode=pl.Buffered(1))


def _deepnorm_ln(res, y, g, b):
    z = DN_ALPHA * res + y
    mu = jnp.mean(z, axis=-1, keepdims=True)
    zc = z - mu
    var = jnp.mean(zc * zc, axis=-1, keepdims=True)
    return zc * lax.rsqrt(var + LN_EPS) * g + b


def _proj_kernel(x_ref, w_ref, *rest, n_extra, epilogue):
    extra = rest[:n_extra]
    o_ref = rest[n_extra]
    xb_ref = rest[n_extra + 1]

    @pl.when(pl.program_id(1) == 0)
    def _():
        xb_ref[...] = x_ref[...].astype(BF16)

    acc = jnp.dot(xb_ref[...], w_ref[...], preferred_element_type=F32)
    epilogue(acc, extra, o_ref)


def _proj(x, w, out_dtype, epilogue, extras=(), extra_specs=(), tm=1024, tn=1024, name="proj"):
    m, k = x.shape
    n = w.shape[1]
    tm = min(tm, m)
    tn = min(tn, n)
    kern = functools.partial(_proj_kernel, n_extra=len(extras), epilogue=epilogue)
    return pl.pallas_call(
        kern,
        grid=(m // tm, n // tn),
        in_specs=[pl.BlockSpec((tm, k), lambda i, j: (i, 0)),
                  pl.BlockSpec((k, tn), lambda i, j: (0, j))] + list(extra_specs),
        out_specs=pl.BlockSpec((tm, tn), lambda i, j: (i, j)),
        out_shape=jax.ShapeDtypeStruct((m, n), out_dtype),
        scratch_shapes=[pltpu.VMEM((tm, k), BF16)],
        compiler_params=_cparams("parallel", "arbitrary"),
        name=name,
    )(x, w, *extras)


def _plain_epilogue(acc, extra, o_ref):
    o_ref[...] = acc.astype(o_ref.dtype)


def _rope_tables(s):
    rows = s // GRID_W
    row_ids = jnp.repeat(jnp.arange(rows), GRID_W).astype(F32)
    col_ids = jnp.tile(jnp.arange(GRID_W), rows).astype(F32)
    axis_dim = HEAD_DIM // 2
    inv_freq = ROPE_THETA ** (-jnp.arange(0, axis_dim, 2, dtype=F32) / axis_dim)
    ang = jnp.concatenate([row_ids[:, None] * inv_freq, col_ids[:, None] * inv_freq], axis=1)
    cos = jnp.concatenate([jnp.cos(ang), jnp.cos(ang)], axis=1)
    sin = jnp.concatenate([-jnp.sin(ang), jnp.sin(ang)], axis=1)
    return cos, sin


def _half_split_heads(a):
    lead = a.shape[:-1]
    a = a.reshape(*lead, -1, 2, 2, HEAD_DIM // 4)
    return jnp.swapaxes(a, -3, -2).reshape(*lead, -1)


def _attn_proj_kernel(x_ref, w_ref, cos_ref, sin_ref, gq_ref, gk_ref, q_ref, kv_ref, xb_ref, acc_ref, *,
                      q_scale, rows):
    j = pl.program_id(1)
    last = pl.num_programs(1) - 1

    @pl.when(j == 0)
    def _():
        xb_ref[...] = x_ref[...].astype(BF16)
        acc_ref[...] = jnp.dot(xb_ref[...], w_ref[...], preferred_element_type=F32)

    @pl.when(j != 0)
    def _():
        acc_ref[...] = jnp.dot(xb_ref[...], w_ref[...], preferred_element_type=F32)

    tm, tn = acc_ref.shape
    ones = jnp.ones((HEAD_DIM, HEAD_DIM), BF16)

    def norm_rope(o_ref, heads, gains_ref, out_scale):
        g = gains_ref[0:1, :]
        gsw = gains_ref[1:2, :]

        def body(r, carry):
            r0 = pl.multiple_of(r * rows, rows)
            gc = cos_ref[pl.ds(r0, rows), :] * g
            gs = sin_ref[pl.ds(r0, rows), :] * gsw
            for h in range(heads):
                sl = slice(h * HEAD_DIM, (h + 1) * HEAD_DIM)
                a = acc_ref[pl.ds(r0, rows), sl]
                ssq = jnp.dot((a * a).astype(BF16), ones, preferred_element_type=F32)
                rr = lax.rsqrt(ssq + HEAD_DIM * RMS_EPS) * (out_scale * HEAD_DIM ** 0.5)
                y = (a * gc + pltpu.roll(a, HEAD_DIM // 2, 1) * gs) * rr
                o_ref[pl.ds(r0, rows), sl] = y.astype(o_ref.dtype)
            return carry

        lax.fori_loop(0, tm // rows, body, 0, unroll=2)

    @pl.when(j < last)
    def _():
        norm_rope(q_ref, tn // HEAD_DIM, gq_ref, q_scale)

    @pl.when(j == last)
    def _():
        k_heads = tn // (2 * HEAD_DIM)
        norm_rope(kv_ref, k_heads, gk_ref, 1.0)
        k_cols = k_heads * HEAD_DIM
        for h in range(k_heads):
            src = slice(k_cols + h * HEAD_DIM, k_cols + (h + 1) * HEAD_DIM)
            dst = k_cols + 2 * h * HEAD_DIM
            kv_ref[:, dst:dst + HEAD_DIM] = acc_ref[:, src].astype(kv_ref.dtype)
            kv_ref[:, dst + HEAD_DIM:dst + 2 * HEAD_DIM] = jnp.ones((tm, HEAD_DIM), kv_ref.dtype)


def _attn_proj(x, w, q_gain, k_gain, tables, seq, q_scale, tm=1024, rows=128):
    m, k = x.shape
    tn = w.shape[1] - k
    tm = min(tm, seq)
    rows = min(rows, tm)
    nblk = seq // tm
    nq = k // tn
    cos, sin = tables

    def gains(gain):
        gain = _half_split_heads(gain).reshape(1, HEAD_DIM)
        return jnp.concatenate([gain, jnp.roll(gain, HEAD_DIM // 2, axis=1)], axis=0)

    kern = functools.partial(_attn_proj_kernel, q_scale=q_scale, rows=rows)
    tab_spec = pl.BlockSpec((tm, HEAD_DIM), lambda i, j: (i % nblk, 0))
    g_spec = pl.BlockSpec((2, HEAD_DIM), lambda i, j: (0, 0))
    return pl.pallas_call(
        kern,
        grid=(m // tm, nq + 1),
        in_specs=[pl.BlockSpec((tm, k), lambda i, j: (i, 0)),
                  pl.BlockSpec((k, tn), lambda i, j: (0, j)),
                  tab_spec, tab_spec, g_spec, g_spec],
        out_specs=[pl.BlockSpec((tm, tn), lambda i, j: (i, jnp.minimum(j, nq - 1))),
                   pl.BlockSpec((tm, 3 * tn // 2), lambda i, j: (i, 0))],
        out_shape=[jax.ShapeDtypeStruct((m, k), BF16),
                   jax.ShapeDtypeStruct((m, 3 * tn // 2), BF16)],
        scratch_shapes=[pltpu.VMEM((tm, k), BF16), pltpu.VMEM((tm, tn), F32)],
        compiler_params=_cparams("parallel", "arbitrary"),
        name="attn_qkv_proj",
    )(x, w, cos, sin, gains(q_gain), gains(k_gain))


def _flash_kernel(q_ref, k_ref, v_ref, o_ref, q4_ref, *, tq, tk):
    seq = k_ref.shape[0]
    for h in range(GQA_GROUP):
        q4_ref[h * tq:(h + 1) * tq, :] = q_ref[:, h * HEAD_DIM:(h + 1) * HEAD_DIM]
    q4 = q4_ref[...]
    rows = GQA_GROUP * tq

    m = jnp.full((rows, 1), -jnp.inf, F32)
    acc = jnp.zeros((rows, 2 * HEAD_DIM), F32)
    for c in range(seq // tk):
        kc = k_ref[c * tk:(c + 1) * tk, :]
        vc = v_ref[c * tk:(c + 1) * tk, :]
        s = lax.dot_general(q4, kc, (((1,), (1,)), ((), ())), preferred_element_type=F32)
        m_new = jnp.maximum(m, jnp.max(s, axis=-1, keepdims=True))
        p = jnp.exp2(s - m_new).astype(BF16)
        acc = jnp.exp2(m - m_new) * acc + jnp.dot(p, vc, preferred_element_type=F32)
        m = m_new
    out = acc[:, :HEAD_DIM] / acc[:, HEAD_DIM:]
    for h in range(GQA_GROUP):
        o_ref[:, h * HEAD_DIM:(h + 1) * HEAD_DIM] = out[h * tq:(h + 1) * tq, :].astype(o_ref.dtype)


def _flash_attention(q, kv, batch, seq, tq=1024, tk=256):
    t, dq = q.shape
    nkv = dq // HEAD_DIM // GQA_GROUP
    tq = min(tq, seq)
    tk = min(tk, seq)
    nq = seq // tq
    gw = GQA_GROUP * HEAD_DIM
    v0 = nkv // 2
    kern = functools.partial(_flash_kernel, tq=tq, tk=tk)
    return pl.pallas_call(
        kern,
        grid=(batch, nkv, nq),
        in_specs=[pl.BlockSpec((tq, gw), lambda b, g, i: (b * nq + i, g)),
                  pl.BlockSpec((seq, HEAD_DIM), lambda b, g, i: (b, g)),
                  pl.BlockSpec((seq, 2 * HEAD_DIM), lambda b, g, i: (b, v0 + g))],
        out_specs=pl.BlockSpec((tq, gw), lambda b, g, i: (b * nq + i, g)),
        out_shape=jax.ShapeDtypeStruct((t, dq), BF16),
        scratch_shapes=[pltpu.VMEM((GQA_GROUP * tq, HEAD_DIM), BF16)],
        compiler_params=_cparams("parallel", "parallel", "arbitrary"),
        name="flash_attention",
    )(q, kv, kv)


def _out_ln_kernel(a_ref, w_ref, res_ref, g_ref, b_ref, o_ref):
    y = jnp.dot(a_ref[...], w_ref[...], preferred_element_type=F32)
    o_ref[...] = _deepnorm_ln(res_ref[...], y, g_ref[...], b_ref[...])


def _out_ln(a, w, res, g, b, tm=512):
    m, k = a.shape
    d = w.shape[1]
    tm = min(tm, m)
    return pl.pallas_call(
        _out_ln_kernel,
        grid=(m // tm,),
        in_specs=[pl.BlockSpec((tm, k), lambda i: (i, 0)),
                  _resident((k, d), lambda i: (0, 0)),
                  pl.BlockSpec((tm, d), lambda i: (i, 0)),
                  _resident((1, d), lambda i: (0, 0)),
                  _resident((1, d), lambda i: (0, 0))],
        out_specs=pl.BlockSpec((tm, d), lambda i: (i, 0)),
        out_shape=jax.ShapeDtypeStruct((m, d), F32),
        compiler_params=_cparams("parallel"),
        name="attn_out_ln",
    )(a, w, res, g.reshape(1, d), b.reshape(1, d))


def _ml_in_kernel(x_ref, w_ref, wg_ref, qkv_ref, gates_ref, xb_ref, *, k_scale):
    j = pl.program_id(1)

    def project():
        acc = jnp.dot(xb_ref[...], w_ref[...], preferred_element_type=F32)
        scale = jnp.where(j == 1, k_scale, 1.0).astype(F32)
        qkv_ref[...] = (acc * scale).astype(qkv_ref.dtype)

    @pl.when(j == 0)
    def _():
        xb_ref[...] = x_ref[...].astype(BF16)
        gates_ref[...] = jnp.dot(xb_ref[...], wg_ref[...], preferred_element_type=F32)
        project()

    @pl.when(j != 0)
    def _():
        project()


def _ml_in_proj(x, w_qkv, w_g, k_scale, tn, tm=1024):
    m, k = x.shape
    n = w_qkv.shape[1]
    tm = min(tm, m)
    kern = functools.partial(_ml_in_kernel, k_scale=k_scale)
    return pl.pallas_call(
        kern,
        grid=(m // tm, n // tn),
        in_specs=[pl.BlockSpec((tm, k), lambda i, j: (i, 0)),
                  pl.BlockSpec((k, tn), lambda i, j: (0, j)),
                  _resident((k, LANES), lambda i, j: (0, 0))],
        out_specs=[pl.BlockSpec((tm, tn), lambda i, j: (i, j)),
                   pl.BlockSpec((tm, LANES), lambda i, j: (i, 0))],
        out_shape=[jax.ShapeDtypeStruct((m, n), BF16), jax.ShapeDtypeStruct((m, LANES), F32)],
        scratch_shapes=[pltpu.VMEM((tm, k), BF16)],
        compiler_params=_cparams("parallel", "arbitrary"),
        name="mlstm_in_proj",
    )(x, w_qkv, w_g)


def _log_sigmoid(x):
    return jnp.minimum(x, 0.0) - jnp.log(1.0 + jnp.exp(-jnp.abs(x)))


def _lane_scan(x, op, fill, backward):
    length = x.shape[1]
    lane = lax.broadcasted_iota(jnp.int32, x.shape, 1)
    k = 1
    while k < length:
        if backward:
            x = op(x, jnp.where(lane < length - k, pltpu.roll(x, length - k, 1), fill))
        else:
            x = op(x, jnp.where(lane >= k, pltpu.roll(x, k, 1), fill))
        k *= 2
    return x


def _gate_prework(gates, backward):
    nh = ML_HEADS
    length = gates.shape[1]
    f_rows = _log_sigmoid(gates[nh:])
    b_rows = _lane_scan(f_rows, jnp.add, 0.0, backward)
    a_rows = gates[:nh] - b_rows
    cm_rows = _lane_scan(a_rows, jnp.maximum, -jnp.inf, backward)
    b_tot = jnp.broadcast_to(jnp.sum(f_rows, axis=1, keepdims=True), (nh, length))
    a_max = jnp.broadcast_to(jnp.max(a_rows, axis=1, keepdims=True), (nh, length))
    rows = jnp.concatenate([a_rows, b_tot, a_max, jnp.zeros((nh, length), F32)], axis=0)
    cols = jnp.concatenate([cm_rows, b_rows, jnp.zeros((length - 2 * nh, length), F32)], axis=0).T
    return rows, cols


def _mlstm_direction(backward, qkv_ref, kt_ref, gn_ref, bias, o_ref, c_ref, m_ref, rows_ref, cols_ref,
                     dqk, dv):
    chunk = qkv_ref.shape[0]
    nh = ML_HEADS
    v0 = 2 * nh * dqk
    rows = rows_ref[...]
    cols = cols_ref[...]
    rows_next, cols_next = _gate_prework(gn_ref[0, 0] + bias, backward)
    rows_ref[...] = rows_next
    cols_ref[...] = cols_next

    row = lax.broadcasted_iota(jnp.int32, (chunk, chunk), 0)
    col = lax.broadcasted_iota(jnp.int32, (chunk, chunk), 1)
    mask = (col >= row) if backward else (col <= row)

    a_rows = rows[:nh]
    m_prev_all = m_ref[:, 0:1]
    m_x_all = jnp.maximum(m_prev_all, rows[2 * nh:3 * nh, 0:1])
    g_s_all = jnp.exp(m_prev_all - m_x_all)
    ws_rows = jnp.exp(a_rows - m_x_all)
    m_ref[...] = jnp.broadcast_to(rows[nh:2 * nh, 0:1] + m_x_all, m_ref.shape)
    ones = jnp.ones((chunk, LANES), BF16)

    for h in range(nh):
        m_prev = m_prev_all[h:h + 1, :]
        cm_col = jnp.broadcast_to(cols[:, h:h + 1], (chunk, LANES))
        b_col = jnp.broadcast_to(cols[:, nh + h:nh + h + 1], (chunk, LANES))
        m_col = jnp.maximum(m_prev, cm_col)
        floor = jnp.exp(-(b_col + m_col))
        w_mat = jnp.exp(jnp.where(mask, a_rows[h:h + 1, :], -jnp.inf) - m_col)
        g_col = jnp.exp(m_prev - m_col)

        qh = qkv_ref[:, h * dqk:(h + 1) * dqk]
        kth = kt_ref[h * dqk:(h + 1) * dqk, :]
        v_ext = jnp.concatenate([qkv_ref[:, v0 + h * dv:v0 + (h + 1) * dv], ones], axis=1)
        ct = c_ref[h]
        s = jnp.dot(qh, kth, preferred_element_type=F32) * w_mat
        inter = jnp.dot(qh, ct.astype(BF16), preferred_element_type=F32)
        intra = jnp.dot(s.astype(BF16), v_ext, preferred_element_type=F32)
        den = jnp.maximum(jnp.abs(g_col * inter[:, dv:] + intra[:, dv:]), floor)
        for part in range(dv // LANES):
            sl = slice(part * LANES, (part + 1) * LANES)
            o_ref[:, h * dv + part * LANES:h * dv + (part + 1) * LANES] = (
                (g_col * inter[:, sl] + intra[:, sl]) / den)

        kts = (kth.astype(F32) * ws_rows[h:h + 1, :]).astype(BF16)
        c_ref[h] = g_s_all[h:h + 1, :] * ct + jnp.dot(kts, v_ext, preferred_element_type=F32)


def _mlstm_kernel(qkvf_ref, ktf_ref, g0f_ref, gnf_ref, qkvb_ref, ktb_ref, g0b_ref, gnb_ref, bias_ref,
                  of_ref, ob_ref, c_ref, m_ref, rows_ref, cols_ref, *, dqk, dv):
    @pl.when(pl.program_id(1) == 0)
    def _():
        c_ref[...] = jnp.zeros_like(c_ref)
        m_ref[...] = jnp.zeros_like(m_ref)
        for d, g_ref in enumerate((g0f_ref, g0b_ref)):
            rows0, cols0 = _gate_prework(g_ref[0, 0] + bias_ref[d], d == 1)
            rows_ref[d] = rows0
            cols_ref[d] = cols0

    dirs = ((qkvf_ref, ktf_ref, gnf_ref, of_ref), (qkvb_ref, ktb_ref, gnb_ref, ob_ref))
    for d, (qkv_ref, kt_ref, gn_ref, o_ref) in enumerate(dirs):
        _mlstm_direction(d == 1, qkv_ref, kt_ref, gn_ref, bias_ref[d], o_ref, c_ref.at[d], m_ref.at[d],
                         rows_ref.at[d], cols_ref.at[d], dqk, dv)


def _mlstm(qkv, kt, gates_rows, bias_rows, batch, seq):
    t = qkv.shape[0]
    dv = qkv.shape[1] // (2 * ML_HEADS)
    dqk = dv // 2
    chunk = ML_CHUNK
    assert chunk == LANES and seq % chunk == 0
    nc = seq // chunk
    wqk = ML_HEADS * dqk
    wv = ML_HEADS * dv

    def specs(d):
        def cb(c):
            return nc - 1 - c if d else c

        def cn(c):
            return cb(jnp.minimum(c + 1, nc - 1))

        return [pl.BlockSpec((chunk, 2 * wqk + wv), lambda b, c: (b * nc + cb(c), 0)),
                pl.BlockSpec((wqk, chunk), lambda b, c: (b, cb(c))),
                pl.BlockSpec((1, 1, 2 * ML_HEADS, chunk), lambda b, c: (b, d, 0, cb(0))),
                pl.BlockSpec((1, 1, 2 * ML_HEADS, chunk), lambda b, c: (b, d, 0, cn(c)))]

    kern = functools.partial(_mlstm_kernel, dqk=dqk, dv=dv)
    return pl.pallas_call(
        kern,
        grid=(batch, nc),
        in_specs=specs(0) + specs(1) + [pl.BlockSpec((2, 2 * ML_HEADS, 1), lambda b, c: (0, 0, 0))],
        out_specs=[pl.BlockSpec((chunk, wv), lambda b, c: (b * nc + c, 0)),
                   pl.BlockSpec((chunk, wv), lambda b, c: (b * nc + nc - 1 - c, 0))],
        out_shape=[jax.ShapeDtypeStruct((t, wv), F32), jax.ShapeDtypeStruct((t, wv), F32)],
        scratch_shapes=[pltpu.VMEM((2, ML_HEADS, dqk, dv + LANES), F32),
                        pltpu.VMEM((2, ML_HEADS, LANES), F32),
                        pltpu.VMEM((2, 4 * ML_HEADS, chunk), F32),
                        pltpu.VMEM((2, chunk, chunk), F32)],
        compiler_params=_cparams("parallel", "arbitrary"),
        name="mlstm_chunks",
    )(qkv, kt, gates_rows, gates_rows, qkv, kt, gates_rows, gates_rows, bias_rows)


def _ml_out_kernel(hf_ref, hb_ref, x_ref, wo_ref, gain_ref, w_ref, g_ref, b_ref, o_ref, a_ref, *, dv):
    x = x_ref[...]
    og = jnp.dot(x.astype(BF16), wo_ref[...], preferred_element_type=F32)
    for h in range(ML_HEADS):
        sl = slice(h * dv, (h + 1) * dv)
        hs = hf_ref[:, sl] + hb_ref[:, sl]
        ms = jnp.mean(hs * hs, axis=-1, keepdims=True)
        hn = hs * lax.rsqrt(ms + RMS_EPS) * gain_ref[:, sl]
        a_ref[:, sl] = (hn * jax.nn.sigmoid(og[:, sl])).astype(BF16)
    y = jnp.dot(a_ref[...], w_ref[...], preferred_element_type=F32)
    o_ref[...] = _deepnorm_ln(x, y, g_ref[...], b_ref[...])


def _ml_out(h_fwd, h_bwd, x, w_ogate, gain, w, g, b, tm=256):
    m, d = h_fwd.shape
    tm = min(tm, m)
    kern = functools.partial(_ml_out_kernel, dv=d // ML_HEADS)
    row = lambda i: (i, 0)
    const = lambda i: (0, 0)
    return pl.pallas_call(
        kern,
        grid=(m // tm,),
        in_specs=[pl.BlockSpec((tm, d), row),
                  pl.BlockSpec((tm, d), row),
                  pl.BlockSpec((tm, d), row),
                  _resident((d, d), const),
                  _resident((1, d), const),
                  _resident((d, d), const),
                  _resident((1, d), const),
                  _resident((1, d), const)],
        out_specs=pl.BlockSpec((tm, d), row),
        out_shape=jax.ShapeDtypeStruct((m, d), F32),
        scratch_shapes=[pltpu.VMEM((tm, d), BF16)],
        compiler_params=_cparams("parallel"),
        name="mlstm_out_ln",
    )(h_fwd, h_bwd, x, w_ogate, gain.reshape(1, d), w, g.reshape(1, d), b.reshape(1, d))


def _xattn_fold_kernel(kv_ref, wq_ref, wo_ref, wqk_ref, vo_ref):
    mem_len = kv_ref.shape[0]
    d = wq_ref.shape[0]
    hd = d // XA_HEADS
    for h in range(XA_HEADS):
        hs = slice(h * hd, (h + 1) * hd)
        ms = slice(h * mem_len, (h + 1) * mem_len)
        wqk_ref[0, :, ms] = lax.dot_general(wq_ref[:, hs], kv_ref[:, hs], (((1,), (1,)), ((), ())),
                                            preferred_element_type=F32).astype(wqk_ref.dtype)
        vo_ref[0, ms, :] = jnp.dot(kv_ref[:, d + h * hd:d + (h + 1) * hd], wo_ref[hs, :],
                                   preferred_element_type=F32).astype(vo_ref.dtype)


def _xattn_fold(kv, wq, wo, batch):
    d = wq.shape[0]
    mem_len = kv.shape[0] // batch
    hm = XA_HEADS * mem_len
    return pl.pallas_call(
        _xattn_fold_kernel,
        grid=(batch,),
        in_specs=[pl.BlockSpec((mem_len, 2 * d), lambda bi: (bi, 0)),
                  _resident((d, d), lambda bi: (0, 0)),
                  _resident((d, d), lambda bi: (0, 0))],
        out_specs=[pl.BlockSpec((1, d, hm), lambda bi: (bi, 0, 0)),
                   pl.BlockSpec((1, hm, d), lambda bi: (bi, 0, 0))],
        out_shape=[jax.ShapeDtypeStruct((batch, d, hm), BF16),
                   jax.ShapeDtypeStruct((batch, hm, d), BF16)],
        compiler_params=_cparams("parallel"),
        name="xattn_fold",
    )(kv, wq, wo)


def _xattn_kernel(x_ref, wqk_ref, vo_ref, g_ref, b_ref, o_ref, p_ref, *, mem_len):
    x = x_ref[...]
    scale = (x.shape[1] // XA_HEADS) ** -0.5
    s = jnp.dot(x.astype(BF16), wqk_ref[0], preferred_element_type=F32) * scale
    for h in range(XA_HEADS):
        ms = slice(h * mem_len, (h + 1) * mem_len)
        sh = s[:, ms]
        e = jnp.exp(sh - jnp.max(sh, axis=-1, keepdims=True))
        p_ref[:, ms] = (e / jnp.sum(e, axis=-1, keepdims=True)).astype(BF16)
    y = jnp.dot(p_ref[...], vo_ref[0], preferred_element_type=F32)
    o_ref[...] = _deepnorm_ln(x, y, g_ref[...], b_ref[...])


def _xattn(x, wqk, vo, g, b, batch, seq, tq=1024):
    t, d = x.shape
    hm = wqk.shape[2]
    tq = min(tq, seq)
    nq = seq // tq
    const = lambda bi, i: (0, 0)
    kern = functools.partial(_xattn_kernel, mem_len=hm // XA_HEADS)
    return pl.pallas_call(
        kern,
        grid=(batch, nq),
        in_specs=[pl.BlockSpec((tq, d), lambda bi, i: (bi * nq + i, 0)),
                  pl.BlockSpec((1, d, hm), lambda bi, i: (bi, 0, 0)),
                  pl.BlockSpec((1, hm, d), lambda bi, i: (bi, 0, 0)),
                  _resident((1, d), const),
                  _resident((1, d), const)],
        out_specs=pl.BlockSpec((tq, d), lambda bi, i: (bi * nq + i, 0)),
        out_shape=jax.ShapeDtypeStruct((t, d), F32),
        scratch_shapes=[pltpu.VMEM((tq, hm), BF16)],
        compiler_params=_cparams("parallel", "arbitrary"),
        name="xattn_sublayer",
    )(x, wqk, vo, g.reshape(1, d), b.reshape(1, d))


def _mlp_kernel(x_ref, w1_ref, w2_ref, g_ref, b_ref, o_ref, xb_ref):
    f = pl.program_id(1)

    def partial_sum():
        h = jnp.dot(xb_ref[...], w1_ref[...], preferred_element_type=F32)
        h = jnp.maximum(h, 0.0)
        return jnp.dot((h * h).astype(BF16), w2_ref[...], preferred_element_type=F32)

    @pl.when(f == 0)
    def _():
        xb_ref[...] = x_ref[...].astype(BF16)
        o_ref[...] = partial_sum()

    @pl.when(f != 0)
    def _():
        o_ref[...] += partial_sum()

    @pl.when(f == pl.num_programs(1) - 1)
    def _():
        rows = min(LN_ROWS, o_ref.shape[0])

        def body(r, carry):
            rs = pl.ds(pl.multiple_of(r * rows, rows), rows)
            o_ref[rs, :] = _deepnorm_ln(x_ref[rs, :], o_ref[rs, :], g_ref[...], b_ref[...])
            return carry

        lax.fori_loop(0, o_ref.shape[0] // rows, body, 0)


def _mlp(x, w1, w2, g, b, tm=1024, tf=1024):
    m, d = x.shape
    dff = w1.shape[1]
    tm = min(tm, m)
    tf = min(tf, dff)
    const = lambda i, f: (0, 0)
    return pl.pallas_call(
        _mlp_kernel,
        grid=(m // tm, dff // tf),
        in_specs=[pl.BlockSpec((tm, d), lambda i, f: (i, 0)),
                  pl.BlockSpec((d, tf), lambda i, f: (0, f)),
                  pl.BlockSpec((tf, d), lambda i, f: (f, 0)),
                  _resident((1, d), const),
                  _resident((1, d), const)],
        out_specs=pl.BlockSpec((tm, d), lambda i, f: (i, 0)),
        out_shape=jax.ShapeDtypeStruct((m, d), F32),
        scratch_shapes=[pltpu.VMEM((tm, d), BF16)],
        compiler_params=_cparams("parallel", "arbitrary"),
        name="mlp_sublayer",
    )(x, w1, w2, g.reshape(1, d), b.reshape(1, d))


def _gqa_sublayer(x, batch, seq, w_in, q_gain, k_gain, w_out, ln_g, ln_b):
    d = x.shape[1]
    tables = _rope_tables(seq)
    nkv = d // HEAD_DIM // GQA_GROUP
    kd = nkv * HEAD_DIM
    w = jnp.concatenate([_half_split_heads(w_in[:, :d + kd]), w_in[:, d + kd:]], axis=1).astype(BF16)
    q_scale = HEAD_DIM ** -0.5 * math.log2(math.e)
    q, kv = _attn_proj(x, w, q_gain, k_gain, tables, seq, q_scale)
    o = _flash_attention(q, kv, batch, seq)
    return _out_ln(o, w_out.astype(BF16), x, ln_g, ln_b)


def _mlstm_sublayer(x, batch, seq, w_in, b_gate, head_gain, w_out, ln_g, ln_b):
    t, d = x.shape
    dv = d // ML_HEADS
    dqk = dv // 2
    nqk = ML_HEADS * dqk
    w_qkv = w_in[:, :2 * nqk + d].astype(BF16)
    w_o = w_in[:, 2 * nqk + d:2 * nqk + 2 * d].astype(BF16)
    ng = 4 * ML_HEADS
    w_g = jnp.pad(w_in[:, 2 * nqk + 2 * d:], ((0, 0), (0, LANES - ng))).astype(BF16)
    qkv, gates = _ml_in_proj(x, w_qkv, w_g, dqk ** -0.5, nqk)
    gates = gates[:, :ng].reshape(batch, seq, 2, 2 * ML_HEADS).transpose(0, 2, 3, 1)
    kt = qkv[:, nqk:2 * nqk].reshape(batch, seq, nqk).transpose(0, 2, 1).reshape(batch * nqk, seq)
    bias = b_gate.astype(F32).reshape(2, 2 * ML_HEADS, 1)
    h_fwd, h_bwd = _mlstm(qkv, kt, gates, bias, batch, seq)
    return _ml_out(h_fwd, h_bwd, x, w_o, head_gain.reshape(-1), w_out.astype(BF16), ln_g, ln_b)


def _trunk(x3, mem3, p):
    batch, seq, d = x3.shape
    x = x3.reshape(batch * seq, d)
    mem = mem3.reshape(-1, d)
    for i in range(DEPTH):
        j = i // 2
        if i % 2 == 0:
            x = _gqa_sublayer(x, batch, seq, p['attn_w_in'][j], p['attn_q_gain'][j], p['attn_k_gain'][j],
                              p['attn_w_out'][j], p['ln_g'][i, 0], p['ln_b'][i, 0])
        else:
            x = _mlstm_sublayer(x, batch, seq, p['ml_w_in'][j], p['ml_b_gate'][j], p['ml_head_gain'][j],
                                p['ml_w_out'][j], p['ln_g'][i, 0], p['ln_b'][i, 0])
        kv = _proj(mem, p['xa_w_kv'][i].astype(BF16), BF16, _plain_epilogue, name="xattn_kv_proj")
        wqk, vo = _xattn_fold(kv, p['xa_w_q'][i].astype(BF16), p['xa_w_out'][i].astype(BF16), batch)
        x = _xattn(x, wqk, vo, p['ln_g'][i, 1], p['ln_b'][i, 1], batch, seq)
        x = _mlp(x, p['mlp_w1'][i].astype(BF16), p['mlp_w2'][i].astype(BF16),
                 p['ln_g'][i, 2], p['ln_b'][i, 2])
    return x.reshape(batch, seq, d)


def kernel(x_prompt, x_sample, mem_prompt, mem_sample, attn_w_in, attn_q_gain, attn_k_gain, attn_w_out,
           ml_w_in, ml_b_gate, ml_head_gain, ml_w_out, xa_w_q, xa_w_kv, xa_w_out, mlp_w1, mlp_w2,
           ln_g, ln_b):
    params = {
        'attn_w_in': attn_w_in, 'attn_q_gain': attn_q_gain, 'attn_k_gain': attn_k_gain,
        'attn_w_out': attn_w_out, 'ml_w_in': ml_w_in, 'ml_b_gate': ml_b_gate,
        'ml_head_gain': ml_head_gain, 'ml_w_out': ml_w_out, 'xa_w_q': xa_w_q, 'xa_w_kv': xa_w_kv,
        'xa_w_out': xa_w_out, 'mlp_w1': mlp_w1, 'mlp_w2': mlp_w2, 'ln_g': ln_g, 'ln_b': ln_b,
    }
    y_prompt = _trunk(x_prompt, mem_prompt, params)
    y_sample = _trunk(x_sample, mem_sample, params)
    return (y_prompt, y_sample)
```

```python
import functools
import math

import jax
import jax.numpy as jnp
from jax import lax
from jax.experimental import pallas as pl
from jax.experimental.pallas import tpu as pltpu

F32 = jnp.float32
BF16 = jnp.bfloat16

DEPTH = 4
HEAD_DIM = 128
GQA_GROUP = 4
GRID_W = 64
ROPE_THETA = 10000.0
ML_HEADS = 8
XA_HEADS = 4
DN_ALPHA = (2 * DEPTH) ** 0.25
LN_EPS = 1e-5
RMS_EPS = 1e-6

V7X_VMEM_BYTES = 64 * 1024 * 1024
VMEM_LIMIT_BYTES = V7X_VMEM_BYTES - 4 * 1024 * 1024
LANES = 128

ML_CHUNK = 128
LN_ROWS = 128


def _cparams(*sem):
    return pltpu.CompilerParams(dimension_semantics=sem, vmem_limit_bytes=VMEM_LIMIT_BYTES)


def _resident(block_shape, index_map):
    return pl.BlockSpec(block_shape, index_map, pipeline_mode=pl.Buffered(1))


def _deepnorm_ln(res, y, g, b):
    z = DN_ALPHA * res + y
    mu = jnp.mean(z, axis=-1, keepdims=True)
    zc = z - mu
    var = jnp.mean(zc * zc, axis=-1, keepdims=True)
    return zc * lax.rsqrt(var + LN_EPS) * g + b


def _proj_kernel(x_ref, w_ref, *rest, n_extra, epilogue):
    extra = rest[:n_extra]
    o_ref = rest[n_extra]
    xb_ref = rest[n_extra + 1]

    @pl.when(pl.program_id(1) == 0)
    def _():
        xb_ref[...] = x_ref[...].astype(BF16)

    acc = jnp.dot(xb_ref[...], w_ref[...], preferred_element_type=F32)
    epilogue(acc, extra, o_ref)


def _proj(x, w, out_dtype, epilogue, extras=(), extra_specs=(), tm=1024, tn=1024, name="proj"):
    m, k = x.shape
    n = w.shape[1]
    tm = min(tm, m)
    tn = min(tn, n)
    kern = functools.partial(_proj_kernel, n_extra=len(extras), epilogue=epilogue)
    return pl.pallas_call(
        kern,
        grid=(m // tm, n // tn),
        in_specs=[pl.BlockSpec((tm, k), lambda i, j: (i, 0)),
                  pl.BlockSpec((k, tn), lambda i, j: (0, j))] + list(extra_specs),
        out_specs=pl.BlockSpec((tm, tn), lambda i, j: (i, j)),
        out_shape=jax.ShapeDtypeStruct((m, n), out_dtype),
        scratch_shapes=[pltpu.VMEM((tm, k), BF16)],
        compiler_params=_cparams("parallel", "arbitrary"),
        name=name,
    )(x, w, *extras)


def _plain_epilogue(acc, extra, o_ref):
    o_ref[...] = acc.astype(o_ref.dtype)


def _rope_tables(s):
    rows = s // GRID_W
    row_ids = jnp.repeat(jnp.arange(rows), GRID_W).astype(F32)
    col_ids = jnp.tile(jnp.arange(GRID_W), rows).astype(F32)
    axis_dim = HEAD_DIM // 2
    inv_freq = ROPE_THETA ** (-jnp.arange(0, axis_dim, 2, dtype=F32) / axis_dim)
    ang = jnp.concatenate([row_ids[:, None] * inv_freq, col_ids[:, None] * inv_freq], axis=1)
    cos = jnp.concatenate([jnp.cos(ang), jnp.cos(ang)], axis=1)
    sin = jnp.concatenate([-jnp.sin(ang), jnp.sin(ang)], axis=1)
    return cos, sin


def _half_split_heads(a):
    lead = a.shape[:-1]
    a = a.reshape(*lead, -1, 2, 2, HEAD_DIM // 4)
    return jnp.swapaxes(a, -3, -2).reshape(*lead, -1)


def _attn_proj_kernel(x_ref, w_ref, cos_ref, sin_ref, gq_ref, gk_ref, q_ref, kv_ref, xb_ref, acc_ref, *,
                      q_scale, rows):
    j = pl.program_id(1)
    last = pl.num_programs(1) - 1

    @pl.when(j == 0)
    def _():
        xb_ref[...] = x_ref[...].astype(BF16)
        acc_ref[...] = jnp.dot(xb_ref[...], w_ref[...], preferred_element_type=F32)

    @pl.when(j != 0)
    def _():
        acc_ref[...] = jnp.dot(xb_ref[...], w_ref[...], preferred_element_type=F32)

    tm, tn = acc_ref.shape
    ones = jnp.ones((HEAD_DIM, HEAD_DIM), BF16)

    def norm_rope(o_ref, heads, gains_ref, out_scale):
        g = gains_ref[0:1, :]
        gsw = gains_ref[1:2, :]

        def body(r, carry):
            r0 = pl.multiple_of(r * rows, rows)
            gc = cos_ref[pl.ds(r0, rows), :] * g
            gs = sin_ref[pl.ds(r0, rows), :] * gsw
            for h in range(heads):
                sl = slice(h * HEAD_DIM, (h + 1) * HEAD_DIM)
                a = acc_ref[pl.ds(r0, rows), sl]
                ssq = jnp.dot((a * a).astype(BF16), ones, preferred_element_type=F32)
                rr = lax.rsqrt(ssq + HEAD_DIM * RMS_EPS) * (out_scale * HEAD_DIM ** 0.5)
                y = (a * gc + pltpu.roll(a, HEAD_DIM // 2, 1) * gs) * rr
                o_ref[pl.ds(r0, rows), sl] = y.astype(o_ref.dtype)
            return carry

        lax.fori_loop(0, tm // rows, body, 0, unroll=2)

    @pl.when(j < last)
    def _():
        norm_rope(q_ref, tn // HEAD_DIM, gq_ref, q_scale)

    @pl.when(j == last)
    def _():
        k_heads = tn // (2 * HEAD_DIM)
        norm_rope(kv_ref, k_heads, gk_ref, 1.0)
        k_cols = k_heads * HEAD_DIM
        for h in range(k_heads):
            src = slice(k_cols + h * HEAD_DIM, k_cols + (h + 1) * HEAD_DIM)
            dst = k_cols + 2 * h * HEAD_DIM
            kv_ref[:, dst:dst + HEAD_DIM] = acc_ref[:, src].astype(kv_ref.dtype)
            kv_ref[:, dst + HEAD_DIM:dst + 2 * HEAD_DIM] = jnp.ones((tm, HEAD_DIM), kv_ref.dtype)


def _attn_proj(x, w, q_gain, k_gain, tables, seq, q_scale, tm=1024, rows=128):
    m, k = x.shape
    tn = w.shape[1] - k
    tm = min(tm, seq)
    rows = min(rows, tm)
    nblk = seq // tm
    nq = k // tn
    cos, sin = tables

    def gains(gain):
        gain = _half_split_heads(gain).reshape(1, HEAD_DIM)
        return jnp.concatenate([gain, jnp.roll(gain, HEAD_DIM // 2, axis=1)], axis=0)

    kern = functools.partial(_attn_proj_kernel, q_scale=q_scale, rows=rows)
    tab_spec = pl.BlockSpec((tm, HEAD_DIM), lambda i, j: (i % nblk, 0))
    g_spec = pl.BlockSpec((2, HEAD_DIM), lambda i, j: (0, 0))
    return pl.pallas_call(
        kern,
        grid=(m // tm, nq + 1),
        in_specs=[pl.BlockSpec((tm, k), lambda i, j: (i, 0)),
                  pl.BlockSpec((k, tn), lambda i, j: (0, j)),
                  tab_spec, tab_spec, g_spec, g_spec],
        out_specs=[pl.BlockSpec((tm, tn), lambda i, j: (i, jnp.minimum(j, nq - 1))),
                   pl.BlockSpec((tm, 3 * tn // 2), lambda i, j: (i, 0))],
        out_shape=[jax.ShapeDtypeStruct((m, k), BF16),
                   jax.ShapeDtypeStruct((m, 3 * tn // 2), BF16)],
        scratch_shapes=[pltpu.VMEM((tm, k), BF16), pltpu.VMEM((tm, tn), F32)],
        compiler_params=_cparams("parallel", "arbitrary"),
        name="attn_qkv_proj",
    )(x, w, cos, sin, gains(q_gain), gains(k_gain))


def _flash_kernel(q_ref, k_ref, v_ref, o_ref, q4_ref, *, tq, tk):
    seq = k_ref.shape[0]
    for h in range(GQA_GROUP):
        q4_ref[h * tq:(h + 1) * tq, :] = q_ref[:, h * HEAD_DIM:(h + 1) * HEAD_DIM]
    q4 = q4_ref[...]
    rows = GQA_GROUP * tq

    m = jnp.full((rows, 1), -jnp.inf, F32)
    acc = jnp.zeros((rows, 2 * HEAD_DIM), F32)
    for c in range(seq // tk):
        kc = k_ref[c * tk:(c + 1) * tk, :]
        vc = v_ref[c * tk:(c + 1) * tk, :]
        s = lax.dot_general(q4, kc, (((1,), (1,)), ((), ())), preferred_element_type=F32)
        m_new = jnp.maximum(m, jnp.max(s, axis=-1, keepdims=True))
        p = jnp.exp2(s - m_new).astype(BF16)
        acc = jnp.exp2(m - m_new) * acc + jnp.dot(p, vc, preferred_element_type=F32)
        m = m_new
    out = acc[:, :HEAD_DIM] / acc[:, HEAD_DIM:]
    for h in range(GQA_GROUP):
        o_ref[:, h * HEAD_DIM:(h + 1) * HEAD_DIM] = out[h * tq:(h + 1) * tq, :].astype(o_ref.dtype)


def _flash_attention(q, kv, batch, seq, tq=1024, tk=256):
    t, dq = q.shape
    nkv = dq // HEAD_DIM // GQA_GROUP
    tq = min(tq, seq)
    tk = min(tk, seq)
    nq = seq // tq
    gw = GQA_GROUP * HEAD_DIM
    v0 = nkv // 2
    kern = functools.partial(_flash_kernel, tq=tq, tk=tk)
    return pl.pallas_call(
        kern,
        grid=(batch, nkv, nq),
        in_specs=[pl.BlockSpec((tq, gw), lambda b, g, i: (b * nq + i, g)),
                  pl.BlockSpec((seq, HEAD_DIM), lambda b, g, i: (b, g)),
                  pl.BlockSpec((seq, 2 * HEAD_DIM), lambda b, g, i: (b, v0 + g))],
        out_specs=pl.BlockSpec((tq, gw), lambda b, g, i: (b * nq + i, g)),
        out_shape=jax.ShapeDtypeStruct((t, dq), BF16),
        scratch_shapes=[pltpu.VMEM((GQA_GROUP * tq, HEAD_DIM), BF16)],
        compiler_params=_cparams("parallel", "parallel", "arbitrary"),
        name="flash_attention",
    )(q, kv, kv)


def _out_ln_kernel(a_ref, w_ref, res_ref, g_ref, b_ref, o_ref):
    y = jnp.dot(a_ref[...], w_ref[...], preferred_element_type=F32)
    o_ref[...] = _deepnorm_ln(res_ref[...], y, g_ref[...], b_ref[...])


def _out_ln(a, w, res, g, b, tm=512):
    m, k = a.shape
    d = w.shape[1]
    tm = min(tm, m)
    return pl.pallas_call(
        _out_ln_kernel,
        grid=(m // tm,),
        in_specs=[pl.BlockSpec((tm, k), lambda i: (i, 0)),
                  _resident((k, d), lambda i: (0, 0)),
                  pl.BlockSpec((tm, d), lambda i: (i, 0)),
                  _resident((1, d), lambda i: (0, 0)),
                  _resident((1, d), lambda i: (0, 0))],
        out_specs=pl.BlockSpec((tm, d), lambda i: (i, 0)),
        out_shape=jax.ShapeDtypeStruct((m, d), F32),
        compiler_params=_cparams("parallel"),
        name="attn_out_ln",
    )(a, w, res, g.reshape(1, d), b.reshape(1, d))


def _ml_in_kernel(x_ref, w_ref, wg_ref, qkv_ref, gates_ref, xb_ref, *, k_scale):
    j = pl.program_id(1)

    def project():
        acc = jnp.dot(xb_ref[...], w_ref[...], preferred_element_type=F32)
        scale = jnp.where(j == 1, k_scale, 1.0).astype(F32)
        qkv_ref[...] = (acc * scale).astype(qkv_ref.dtype)

    @pl.when(j == 0)
    def _():
        xb_ref[...] = x_ref[...].astype(BF16)
        gates_ref[...] = jnp.dot(xb_ref[...], wg_ref[...], preferred_element_type=F32)
        project()

    @pl.when(j != 0)
    def _():
        project()


def _ml_in_proj(x, w_qkv, w_g, k_scale, tn, tm=1024):
    m, k = x.shape
    n = w_qkv.shape[1]
    tm = min(tm, m)
    kern = functools.partial(_ml_in_kernel, k_scale=k_scale)
    return pl.pallas_call(
        kern,
        grid=(m // tm, n // tn),
        in_specs=[pl.BlockSpec((tm, k), lambda i, j: (i, 0)),
                  pl.BlockSpec((k, tn), lambda i, j: (0, j)),
                  _resident((k, LANES), lambda i, j: (0, 0))],
        out_specs=[pl.BlockSpec((tm, tn), lambda i, j: (i, j)),
                   pl.BlockSpec((tm, LANES), lambda i, j: (i, 0))],
        out_shape=[jax.ShapeDtypeStruct((m, n), BF16), jax.ShapeDtypeStruct((m, LANES), F32)],
        scratch_shapes=[pltpu.VMEM((tm, k), BF16)],
        compiler_params=_cparams("parallel", "arbitrary"),
        name="mlstm_in_proj",
    )(x, w_qkv, w_g)


def _log_sigmoid(x):
    return jnp.minimum(x, 0.0) - jnp.log(1.0 + jnp.exp(-jnp.abs(x)))


def _lane_scan(x, op, fill, backward):
    length = x.shape[1]
    lane = lax.broadcasted_iota(jnp.int32, x.shape, 1)
    k = 1
    while k < length:
        if backward:
            x = op(x, jnp.where(lane < length - k, pltpu.roll(x, length - k, 1), fill))
        else:
            x = op(x, jnp.where(lane >= k, pltpu.roll(x, k, 1), fill))
        k *= 2
    return x


def _gate_prework(gates, backward):
    nh = ML_HEADS
    length = gates.shape[1]
    f_rows = _log_sigmoid(gates[nh:])
    b_rows = _lane_scan(f_rows, jnp.add, 0.0, backward)
    a_rows = gates[:nh] - b_rows
    cm_rows = _lane_scan(a_rows, jnp.maximum, -jnp.inf, backward)
    b_tot = jnp.broadcast_to(jnp.sum(f_rows, axis=1, keepdims=True), (nh, length))
    a_max = jnp.broadcast_to(jnp.max(a_rows, axis=1, keepdims=True), (nh, length))
    rows = jnp.concatenate([a_rows, b_tot, a_max, jnp.zeros((nh, length), F32)], axis=0)
    cols = jnp.concatenate([cm_rows, b_rows, jnp.zeros((length - 2 * nh, length), F32)], axis=0).T
    return rows, cols


def _mlstm_direction(backward, qkv_ref, kt_ref, gn_ref, bias, o_ref, c_ref, m_ref, rows_ref, cols_ref,
                     dqk, dv):
    chunk = qkv_ref.shape[0]
    nh = ML_HEADS
    v0 = 2 * nh * dqk
    rows = rows_ref[...]
    cols = cols_ref[...]
    rows_next, cols_next = _gate_prework(gn_ref[0, 0] + bias, backward)
    rows_ref[...] = rows_next
    cols_ref[...] = cols_next

    row = lax.broadcasted_iota(jnp.int32, (chunk, chunk), 0)
    col = lax.broadcasted_iota(jnp.int32, (chunk, chunk), 1)
    mask = (col >= row) if backward else (col <= row)

    a_rows = rows[:nh]
    m_prev_all = m_ref[:, 0:1]
    m_x_all = jnp.maximum(m_prev_all, rows[2 * nh:3 * nh, 0:1])
    g_s_all = jnp.exp(m_prev_all - m_x_all)
    ws_rows = jnp.exp(a_rows - m_x_all)
    m_ref[...] = jnp.broadcast_to(rows[nh:2 * nh, 0:1] + m_x_all, m_ref.shape)
    ones = jnp.ones((chunk, LANES), BF16)

    for h in range(nh):
        m_prev = m_prev_all[h:h + 1, :]
        cm_col = jnp.broadcast_to(cols[:, h:h + 1], (chunk, LANES))
        b_col = jnp.broadcast_to(cols[:, nh + h:nh + h + 1], (chunk, LANES))
        m_col = jnp.maximum(m_prev, cm_col)
        floor = jnp.exp(-(b_col + m_col))
        w_mat = jnp.exp(jnp.where(mask, a_rows[h:h + 1, :], -jnp.inf) - m_col)
        g_col = jnp.exp(m_prev - m_col)

        qh = qkv_ref[:, h * dqk:(h + 1) * dqk]
        kth = kt_ref[h * dqk:(h + 1) * dqk, :]
        v_ext = jnp.concatenate([qkv_ref[:, v0 + h * dv:v0 + (h + 1) * dv], ones], axis=1)
        ct = c_ref[h]
        s = jnp.dot(qh, kth, preferred_element_type=F32) * w_mat
        inter = jnp.dot(qh, ct.astype(BF16), preferred_element_type=F32)
        intra = jnp.dot(s.astype(BF16), v_ext, preferred_element_type=F32)
        den = jnp.maximum(jnp.abs(g_col * inter[:, dv:] + intra[:, dv:]), floor)
        for part in range(dv // LANES):
            sl = slice(part * LANES, (part + 1) * LANES)
            o_ref[:, h * dv + part * LANES:h * dv + (part + 1) * LANES] = (
                (g_col * inter[:, sl] + intra[:, sl]) / den)

        kts = (kth.astype(F32) * ws_rows[h:h + 1, :]).astype(BF16)
        c_ref[h] = g_s_all[h:h + 1, :] * ct + jnp.dot(kts, v_ext, preferred_element_type=F32)


def _mlstm_kernel(qkvf_ref, ktf_ref, g0f_ref, gnf_ref, qkvb_ref, ktb_ref, g0b_ref, gnb_ref, bias_ref,
                  of_ref, ob_ref, c_ref, m_ref, rows_ref, cols_ref, *, dqk, dv):
    @pl.when(pl.program_id(1) == 0)
    def _():
        c_ref[...] = jnp.zeros_like(c_ref)
        m_ref[...] = jnp.zeros_like(m_ref)
        for d, g_ref in enumerate((g0f_ref, g0b_ref)):
            rows0, cols0 = _gate_prework(g_ref[0, 0] + bias_ref[d], d == 1)
            rows_ref[d] = rows0
            cols_ref[d] = cols0

    dirs = ((qkvf_ref, ktf_ref, gnf_ref, of_ref), (qkvb_ref, ktb_ref, gnb_ref, ob_ref))
    for d, (qkv_ref, kt_ref, gn_ref, o_ref) in enumerate(dirs):
        _mlstm_direction(d == 1, qkv_ref, kt_ref, gn_ref, bias_ref[d], o_ref, c_ref.at[d], m_ref.at[d],
                         rows_ref.at[d], cols_ref.at[d], dqk, dv)


def _mlstm(qkv, kt, gates_rows, bias_rows, batch, seq):
    t = qkv.shape[0]
    dv = qkv.shape[1] // (2 * ML_HEADS)
    dqk = dv // 2
    chunk = ML_CHUNK
    assert chunk == LANES and seq % chunk == 0
    nc = seq // chunk
    wqk = ML_HEADS * dqk
    wv = ML_HEADS * dv

    def specs(d):
        def cb(c):
            return nc - 1 - c if d else c

        def cn(c):
            return cb(jnp.minimum(c + 1, nc - 1))

        return [pl.BlockSpec((chunk, 2 * wqk + wv), lambda b, c: (b * nc + cb(c), 0)),
                pl.BlockSpec((wqk, chunk), lambda b, c: (b, cb(c))),
                pl.BlockSpec((1, 1, 2 * ML_HEADS, chunk), lambda b, c: (b, d, 0, cb(0))),
                pl.BlockSpec((1, 1, 2 * ML_HEADS, chunk), lambda b, c: (b, d, 0, cn(c)))]

    kern = functools.partial(_mlstm_kernel, dqk=dqk, dv=dv)
    return pl.pallas_call(
        kern,
        grid=(batch, nc),
        in_specs=specs(0) + specs(1) + [pl.BlockSpec((2, 2 * ML_HEADS, 1), lambda b, c: (0, 0, 0))],
        out_specs=[pl.BlockSpec((chunk, wv), lambda b, c: (b * nc + c, 0)),
                   pl.BlockSpec((chunk, wv), lambda b, c: (b * nc + nc - 1 - c, 0))],
        out_shape=[jax.ShapeDtypeStruct((t, wv), F32), jax.ShapeDtypeStruct((t, wv), F32)],
        scratch_shapes=[pltpu.VMEM((2, ML_HEADS, dqk, dv + LANES), F32),
                        pltpu.VMEM((2, ML_HEADS, LANES), F32),
                        pltpu.VMEM((2, 4 * ML_HEADS, chunk), F32),
                        pltpu.VMEM((2, chunk, chunk), F32)],
        compiler_params=_cparams("parallel", "arbitrary"),
        name="mlstm_chunks",
    )(qkv, kt, gates_rows, gates_rows, qkv, kt, gates_rows, gates_rows, bias_rows)


def _ml_out_kernel(hf_ref, hb_ref, x_ref, wo_ref, gain_ref, w_ref, g_ref, b_ref, o_ref, a_ref, *, dv):
    x = x_ref[...]
    og = jnp.dot(x.astype(BF16), wo_ref[...], preferred_element_type=F32)
    for h in range(ML_HEADS):
        sl = slice(h * dv, (h + 1) * dv)
        hs = hf_ref[:, sl] + hb_ref[:, sl]
        ms = jnp.mean(hs * hs, axis=-1, keepdims=True)
        hn = hs * lax.rsqrt(ms + RMS_EPS) * gain_ref[:, sl]
        a_ref[:, sl] = (hn * jax.nn.sigmoid(og[:, sl])).astype(BF16)
    y = jnp.dot(a_ref[...], w_ref[...], preferred_element_type=F32)
    o_ref[...] = _deepnorm_ln(x, y, g_ref[...], b_ref[...])


def _ml_out(h_fwd, h_bwd, x, w_ogate, gain, w, g, b, tm=256):
    m, d = h_fwd.shape
    tm = min(tm, m)
    kern = functools.partial(_ml_out_kernel, dv=d // ML_HEADS)
    row = lambda i: (i, 0)
    const = lambda i: (0, 0)
    return pl.pallas_call(
        kern,
        grid=(m // tm,),
        in_specs=[pl.BlockSpec((tm, d), row),
                  pl.BlockSpec((tm, d), row),
                  pl.BlockSpec((tm, d), row),
                  _resident((d, d), const),
                  _resident((1, d), const),
                  _resident((d, d), const),
                  _resident((1, d), const),
                  _resident((1, d), const)],
        out_specs=pl.BlockSpec((tm, d), row),
        out_shape=jax.ShapeDtypeStruct((m, d), F32),
        scratch_shapes=[pltpu.VMEM((tm, d), BF16)],
        compiler_params=_cparams("parallel"),
        name="mlstm_out_ln",
    )(h_fwd, h_bwd, x, w_ogate, gain.reshape(1, d), w, g.reshape(1, d), b.reshape(1, d))


def _xattn_fold_kernel(mem_ref, wkv_ref, wq_ref, wo_ref, wqk_ref, vo_ref):
    j = pl.program_id(1)
    mem_len = mem_ref.shape[0]
    d = wq_ref.shape[0]
    hd = d // XA_HEADS
    kv = jnp.dot(mem_ref[...].astype(BF16), wkv_ref[...], preferred_element_type=F32).astype(BF16)

    @pl.when(j == 0)
    def _():
        for h in range(XA_HEADS):
            hs = slice(h * hd, (h + 1) * hd)
            wqk_ref[0, :, h * mem_len:(h + 1) * mem_len] = lax.dot_general(
                wq_ref[:, hs], kv[:, hs], (((1,), (1,)), ((), ())),
                preferred_element_type=F32).astype(wqk_ref.dtype)

    @pl.when(j == 1)
    def _():
        for h in range(XA_HEADS):
            hs = slice(h * hd, (h + 1) * hd)
            vo_ref[0, h * mem_len:(h + 1) * mem_len, :] = jnp.dot(
                kv[:, hs], wo_ref[hs, :], preferred_element_type=F32).astype(vo_ref.dtype)


def _xattn_fold(mem, wkv, wq, wo, batch):
    d = wq.shape[0]
    mem_len = mem.shape[0] // batch
    hm = XA_HEADS * mem_len
    return pl.pallas_call(
        _xattn_fold_kernel,
        grid=(batch, 2),
        in_specs=[pl.BlockSpec((mem_len, d), lambda bi, j: (bi, 0)),
                  pl.BlockSpec((d, d), lambda bi, j: (0, j)),
                  _resident((d, d), lambda bi, j: (0, 0)),
                  _resident((d, d), lambda bi, j: (0, 0))],
        out_specs=[pl.BlockSpec((1, d, hm), lambda bi, j: (bi, 0, 0)),
                   pl.BlockSpec((1, hm, d), lambda bi, j: (bi, 0, 0))],
        out_shape=[jax.ShapeDtypeStruct((batch, d, hm), BF16),
                   jax.ShapeDtypeStruct((batch, hm, d), BF16)],
        compiler_params=_cparams("parallel", "arbitrary"),
        name="xattn_fold",
    )(mem, wkv, wq, wo)


def _xattn_kernel(x_ref, wqk_ref, vo_ref, g_ref, b_ref, o_ref, p_ref, *, mem_len):
    x = x_ref[...]
    scale = (x.shape[1] // XA_HEADS) ** -0.5
    s = jnp.dot(x.astype(BF16), wqk_ref[0], preferred_element_type=F32) * scale
    for h in range(XA_HEADS):
        ms = slice(h * mem_len, (h + 1) * mem_len)
        sh = s[:, ms]
        e = jnp.exp(sh - jnp.max(sh, axis=-1, keepdims=True))
        p_ref[:, ms] = (e / jnp.sum(e, axis=-1, keepdims=True)).astype(BF16)
    y = jnp.dot(p_ref[...], vo_ref[0], preferred_element_type=F32)
    o_ref[...] = _deepnorm_ln(x, y, g_ref[...], b_ref[...])


def _xattn(x, wqk, vo, g, b, batch, seq, tq=1024):
    t, d = x.shape
    hm = wqk.shape[2]
    tq = min(tq, seq)
    nq = seq // tq
    const = lambda bi, i: (0, 0)
    kern = functools.partial(_xattn_kernel, mem_len=hm // XA_HEADS)
    return pl.pallas_call(
        kern,
        grid=(batch, nq),
        in_specs=[pl.BlockSpec((tq, d), lambda bi, i: (bi * nq + i, 0)),
                  pl.BlockSpec((1, d, hm), lambda bi, i: (bi, 0, 0)),
                  pl.BlockSpec((1, hm, d), lambda bi, i: (bi, 0, 0)),
                  _resident((1, d), const),
                  _resident((1, d), const)],
        out_specs=pl.BlockSpec((tq, d), lambda bi, i: (bi * nq + i, 0)),
        out_shape=jax.ShapeDtypeStruct((t, d), F32),
        scratch_shapes=[pltpu.VMEM((tq, hm), BF16)],
        compiler_params=_cparams("parallel", "arbitrary"),
        name="xattn_sublayer",
    )(x, wqk, vo, g.reshape(1, d), b.reshape(1, d))


def _mlp_kernel(x_ref, w1_ref, w2_ref, g_ref, b_ref, o_ref, xb_ref):
    f = pl.program_id(1)

    def partial_sum():
        h = jnp.dot(xb_ref[...], w1_ref[...], preferred_element_type=F32)
        h = jnp.maximum(h, 0.0)
        return jnp.dot((h * h).astype(BF16), w2_ref[...], preferred_element_type=F32)

    @pl.when(f == 0)
    def _():
        xb_ref[...] = x_ref[...].astype(BF16)
        o_ref[...] = partial_sum()

    @pl.when(f != 0)
    def _():
        o_ref[...] += partial_sum()

    @pl.when(f == pl.num_programs(1) - 1)
    def _():
        rows = min(LN_ROWS, o_ref.shape[0])

        def body(r, carry):
            rs = pl.ds(pl.multiple_of(r * rows, rows), rows)
            o_ref[rs, :] = _deepnorm_ln(x_ref[rs, :], o_ref[rs, :], g_ref[...], b_ref[...])
            return carry

        lax.fori_loop(0, o_ref.shape[0] // rows, body, 0)


def _mlp(x, w1, w2, g, b, tm=1024, tf=1024):
    m, d = x.shape
    dff = w1.shape[1]
    tm = min(tm, m)
    tf = min(tf, dff)
    const = lambda i, f: (0, 0)
    return pl.pallas_call(
        _mlp_kernel,
        grid=(m // tm, dff // tf),
        in_specs=[pl.BlockSpec((tm, d), lambda i, f: (i, 0)),
                  pl.BlockSpec((d, tf), lambda i, f: (0, f)),
                  pl.BlockSpec((tf, d), lambda i, f: (f, 0)),
                  _resident((1, d), const),
                  _resident((1, d), const)],
        out_specs=pl.BlockSpec((tm, d), lambda i, f: (i, 0)),
        out_shape=jax.ShapeDtypeStruct((m, d), F32),
        scratch_shapes=[pltpu.VMEM((tm, d), BF16)],
        compiler_params=_cparams("parallel", "arbitrary"),
        name="mlp_sublayer",
    )(x, w1, w2, g.reshape(1, d), b.reshape(1, d))


def _gqa_sublayer(x, batch, seq, w_in, q_gain, k_gain, w_out, ln_g, ln_b):
    d = x.shape[1]
    tables = _rope_tables(seq)
    nkv = d // HEAD_DIM // GQA_GROUP
    kd = nkv * HEAD_DIM
    w = jnp.concatenate([_half_split_heads(w_in[:, :d + kd]), w_in[:, d + kd:]], axis=1).astype(BF16)
    q_scale = HEAD_DIM ** -0.5 * math.log2(math.e)
    q, kv = _attn_proj(x, w, q_gain, k_gain, tables, seq, q_scale)
    o = _flash_attention(q, kv, batch, seq)
    return _out_ln(o, w_out.astype(BF16), x, ln_g, ln_b)


def _mlstm_sublayer(x, batch, seq, w_in, b_gate, head_gain, w_out, ln_g, ln_b):
    t, d = x.shape
    dv = d // ML_HEADS
    dqk = dv // 2
    nqk = ML_HEADS * dqk
    w_qkv = w_in[:, :2 * nqk + d].astype(BF16)
    w_o = w_in[:, 2 * nqk + d:2 * nqk + 2 * d].astype(BF16)
    ng = 4 * ML_HEADS
    w_g = jnp.pad(w_in[:, 2 * nqk + 2 * d:], ((0, 0), (0, LANES - ng))).astype(BF16)
    qkv, gates = _ml_in_proj(x, w_qkv, w_g, dqk ** -0.5, nqk)
    gates = gates[:, :ng].reshape(batch, seq, 2, 2 * ML_HEADS).transpose(0, 2, 3, 1)
    kt = qkv[:, nqk:2 * nqk].reshape(batch, seq, nqk).transpose(0, 2, 1).reshape(batch * nqk, seq)
    bias = b_gate.astype(F32).reshape(2, 2 * ML_HEADS, 1)
    h_fwd, h_bwd = _mlstm(qkv, kt, gates, bias, batch, seq)
    return _ml_out(h_fwd, h_bwd, x, w_o, head_gain.reshape(-1), w_out.astype(BF16), ln_g, ln_b)


def _trunk(x3, mem3, p):
    batch, seq, d = x3.shape
    x = x3.reshape(batch * seq, d)
    mem = mem3.reshape(-1, d)
    for i in range(DEPTH):
        j = i // 2
        if i % 2 == 0:
            x = _gqa_sublayer(x, batch, seq, p['attn_w_in'][j], p['attn_q_gain'][j], p['attn_k_gain'][j],
                              p['attn_w_out'][j], p['ln_g'][i, 0], p['ln_b'][i, 0])
        else:
            x = _mlstm_sublayer(x, batch, seq, p['ml_w_in'][j], p['ml_b_gate'][j], p['ml_head_gain'][j],
                                p['ml_w_out'][j], p['ln_g'][i, 0], p['ln_b'][i, 0])
        wqk, vo = _xattn_fold(mem, p['xa_w_kv'][i].astype(BF16), p['xa_w_q'][i].astype(BF16),
                              p['xa_w_out'][i].astype(BF16), batch)
        x = _xattn(x, wqk, vo, p['ln_g'][i, 1], p['ln_b'][i, 1], batch, seq)
        x = _mlp(x, p['mlp_w1'][i].astype(BF16), p['mlp_w2'][i].astype(BF16),
                 p['ln_g'][i, 2], p['ln_b'][i, 2])
    return x.reshape(batch, seq, d)


def kernel(x_prompt, x_sample, mem_prompt, mem_sample, attn_w_in, attn_q_gain, attn_k_gain, attn_w_out,
           ml_w_in, ml_b_gate, ml_head_gain, ml_w_out, xa_w_q, xa_w_kv, xa_w_out, mlp_w1, mlp_w2,
           ln_g, ln_b):
    params = {
        'attn_w_in': attn_w_in, 'attn_q_gain': attn_q_gain, 'attn_k_gain': attn_k_gain,
        'attn_w_out': attn_w_out, 'ml_w_in': ml_w_in, 'ml_b_gate': ml_b_gate,
        'ml_head_gain': ml_head_gain, 'ml_w_out': ml_w_out, 'xa_w_q': xa_w_q, 'xa_w_kv': xa_w_kv,
        'xa_w_out': xa_w_out, 'mlp_w1': mlp_w1, 'mlp_w2': mlp_w2, 'ln_g': ln_g, 'ln_b': ln_b,
    }
    y_prompt = _trunk(x_prompt, mem_prompt, params)
    y_sample = _trunk(x_sample, mem_sample, params)
    return (y_prompt, y_sample)
```
